```python
import math
import jax, jax.numpy as jnp
from jax import lax
import numpy as np

D_MODEL = 1024
BATCH = 8
SEQ = 8192
DEPTH = 4
DEC_BATCH = 4
DEC_SEQ = 8192
PAST_LEN = 128

D_MIX = D_MODEL
D_HYENA = D_MIX // 2
D_ATTN = D_MIX - D_HYENA
ATTN_HEADS = 4
ATTN_HEAD_DIM = D_ATTN // (2 * ATTN_HEADS)
D_IN = 3 * D_HYENA + 3 * D_ATTN
SHORT_CONV = 3
FILTER_EMB = 33
FILTER_WIDTH = 64
DECAY_TARGET = 1e-2
FAST_DECAY_PCT = 0.3
SLOW_DECAY_PCT = 1.5
REL_BUCKETS = 32
REL_MAX_DIST = 128
Q_BLOCK = 128
N_GROUPS = 4
EXPERTS_PER_GROUP = 8
N_EXPERTS = N_GROUPS * EXPERTS_PER_GROUP
TOP_K = 2
D_EXPERT = 512
MOE_BLOCK = 256
LN_EPS = 1e-5
DEEPNORM_ALPHA = (2.0 * DEPTH) ** 0.25
DEEPNORM_BETA = (8.0 * DEPTH) ** -0.25

kernel_name = "hyena_diffattn_hmoe_encoder"

F32 = jnp.float32


def layer_norm(x, g, b):
    xf = x.astype(F32)
    mu = jnp.mean(xf, axis=-1, keepdims=True)
    var = jnp.mean(jnp.square(xf - mu), axis=-1, keepdims=True)
    return ((xf - mu) * lax.rsqrt(var + LN_EPS) * g.astype(F32) + b.astype(F32)).astype(x.dtype)


def short_conv(u, w, b):
    L = u.shape[1]
    half = SHORT_CONV // 2
    up = jnp.pad(u, ((0, 0), (half, half), (0, 0)))
    y = b
    for t in range(SHORT_CONV):
        y = y + up[:, t:t + L] * w[t]
    return y


def hyena_filter(L, f_w1, f_b1, f_freq, f_w2, f_b2, f_w3):
    bands = (FILTER_EMB - 1) // 2
    t = jnp.linspace(0.0, 1.0, L, dtype=F32)[:, None]
    w = (2.0 * math.pi / L) * jnp.arange(L, dtype=F32)[:, None]
    f = jnp.linspace(1e-4, bands - 1, bands, dtype=F32)[None, :]
    z = jnp.concatenate([t, jnp.cos(f * w), -jnp.sin(f * w)], axis=-1)
    freq = f_freq.astype(F32)
    h = jnp.sin(freq * (z @ f_w1.astype(F32) + f_b1.astype(F32)))
    h = jnp.sin(freq * (h @ f_w2.astype(F32) + f_b2.astype(F32)))
    h = h @ f_w3.astype(F32)
    max_decay = math.log(DECAY_TARGET) / FAST_DECAY_PCT
    min_decay = math.log(DECAY_TARGET) / SLOW_DECAY_PCT
    deltas = jnp.abs(jnp.linspace(min_decay, max_decay, D_HYENA, dtype=F32))
    decay = jnp.exp(-t * deltas)
    h_fwd = h[:, :D_HYENA] * decay
    h_bwd = h[:, D_HYENA:] * decay
    k_full = jnp.concatenate([h_fwd, jnp.zeros((1, D_HYENA), F32), h_bwd[1:][::-1]], axis=0)
    return k_full / jnp.sum(jnp.abs(k_full), axis=0, keepdims=True)


def bidir_long_conv(u, k_full):
    L = u.shape[1]
    uf = jnp.fft.rfft(u.astype(F32), n=2 * L, axis=1)
    kf = jnp.fft.rfft(k_full, n=2 * L, axis=0)
    y = jnp.fft.irfft(uf * kf[None], n=2 * L, axis=1)[:, :L]
    return y.astype(u.dtype)


def hyena_mixer(u, conv_w, conv_b, f_w1, f_b1, f_freq, f_w2, f_b2, f_w3, h_bias):
    u = short_conv(u, conv_w, conv_b)
    x0, x1, v = jnp.split(u, 3, axis=-1)
    v = v * x1
    k_full = hyena_filter(u.shape[1], f_w1, f_b1, f_freq, f_w2, f_b2, f_w3)
    v = bidir_long_conv(v, k_full) + v * h_bias
    return v * x0


def rel_buckets(rel):
    nb = REL_BUCKETS // 2
    ret = jnp.where(rel > 0, nb, 0)
    n = jnp.abs(rel)
    max_exact = nb // 2
    large = max_exact + (jnp.log(jnp.maximum(n, 1).astype(F32) / max_exact)
                         / math.log(REL_MAX_DIST / max_exact) * (nb - max_exact)).astype(jnp.int32)
    large = jnp.minimum(large, nb - 1)
    return ret + jnp.where(n < max_exact, n, large)


def diff_attention(q, k, v, lam, lam_init, subln_g, rel_table):
    B, L = q.shape[0], q.shape[1]
    n_blocks = L // Q_BLOCK
    scale = ATTN_HEAD_DIM ** -0.5
    kpos = jnp.arange(L, dtype=jnp.int32)
    table = rel_table.astype(F32)
    g = subln_g.astype(F32)

    def block(i):
        start = i * Q_BLOCK
        qb = lax.dynamic_slice_in_dim(q, start, Q_BLOCK, axis=1)
        s = jnp.einsum('bqhjd,bkhjd->bhjqk', qb, k).astype(F32) * scale
        qpos = start + jnp.arange(Q_BLOCK, dtype=jnp.int32)
        bias = table[rel_buckets(kpos[None, :] - qpos[:, None])]
        s = s + jnp.transpose(bias, (2, 0, 1))[None, :, None]
        p = jax.nn.softmax(s, axis=-1)
        a = p[:, :, 0] - lam * p[:, :, 1]
        o = jnp.einsum('bhqk,bkhe->bqhe', a.astype(v.dtype), v).astype(F32)
        o = o * lax.rsqrt(jnp.mean(o * o, axis=-1, keepdims=True) + LN_EPS) * g * (1.0 - lam_init)
        return o.astype(v.dtype)

    out = lax.map(block, jnp.arange(n_blocks, dtype=jnp.int32))
    return jnp.transpose(out, (1, 0, 2, 3, 4)).reshape(B, L, ATTN_HEADS * 2 * ATTN_HEAD_DIM)


def hier_moe(x, w_route_group, w_route_expert, w_gate, w_up, w_down):
    B, L, D = x.shape
    T = B * L
    xt = x.reshape(T, D)
    glog = (xt @ w_route_group).astype(F32)
    gprob = jax.nn.softmax(glog, axis=-1)
    g_sel = jnp.argmax(glog, axis=-1).astype(jnp.int32)
    g_w = jnp.take_along_axis(gprob, g_sel[:, None], axis=1)
    elog = (xt @ w_route_expert).astype(F32).reshape(T, N_GROUPS, EXPERTS_PER_GROUP)
    elog = jnp.take_along_axis(elog, g_sel[:, None, None], axis=1)[:, 0]
    top_p, top_i = lax.top_k(jax.nn.softmax(elog, axis=-1), TOP_K)
    gates = g_w * top_p / jnp.sum(top_p, axis=-1, keepdims=True)
    eid = g_sel[:, None] * EXPERTS_PER_GROUP + top_i.astype(jnp.int32)
    A = T * TOP_K
    flat_e = eid.reshape(A)
    flat_tok = jnp.arange(A, dtype=jnp.int32) // TOP_K
    order = jnp.argsort(flat_e)
    se = flat_e[order]
    stok = flat_tok[order]
    sg = gates.reshape(A)[order]
    counts = jnp.bincount(flat_e, length=N_EXPERTS).astype(jnp.int32)
    starts = jnp.cumsum(counts) - counts
    pcounts = (counts + MOE_BLOCK - 1) // MOE_BLOCK * MOE_BLOCK
    pends = jnp.cumsum(pcounts)
    pstarts = pends - pcounts
    dest = pstarts[se] + (jnp.arange(A, dtype=jnp.int32) - starts[se])
    n_blocks = -(-A // MOE_BLOCK) + N_EXPERTS
    P = n_blocks * MOE_BLOCK
    row_tok = jnp.full((P,), T, jnp.int32).at[dest].set(stok)
    x_pad = jnp.concatenate([xt, jnp.zeros((1, D), xt.dtype)], axis=0)
    xs = x_pad[row_tok].reshape(n_blocks, MOE_BLOCK, D)
    block_e = jnp.minimum(jnp.searchsorted(pends, jnp.arange(n_blocks, dtype=jnp.int32) * MOE_BLOCK,
                                           side='right'), N_EXPERTS - 1).astype(jnp.int32)

    def expert_block(args):
        xb, e = args
        h = jax.nn.silu(xb @ w_gate[e]) * (xb @ w_up[e])
        return h @ w_down[e]

    ys = lax.map(expert_block, (xs, block_e)).reshape(P, D)
    y = jax.ops.segment_sum(ys[dest] * sg[:, None].astype(ys.dtype), stok, num_segments=T)
    return y.reshape(B, L, D)


def trunk(x, ln_emb_g, ln_emb_b, rel_table, w_in, conv_w, conv_b, f_w1, f_b1, f_freq, f_w2, f_b2,
          f_w3, h_bias, lam_q1, lam_k1, lam_q2, lam_k2, subln_g, w_out, ln1_g, ln1_b,
          w_route_group, w_route_expert, w_gate, w_up, w_down, ln2_g, ln2_b):
    B, L, _ = x.shape
    x = layer_norm(x, ln_emb_g, ln_emb_b)
    for l in range(DEPTH):
        lam_init = 0.8 - 0.6 * math.exp(-0.3 * l)
        u = x @ w_in[l]
        y_h = hyena_mixer(u[..., :3 * D_HYENA], conv_w[l], conv_b[l], f_w1[l], f_b1[l], f_freq[l],
                          f_w2[l], f_b2[l], f_w3[l], h_bias[l])
        q, k, v = jnp.split(u[..., 3 * D_HYENA:], 3, axis=-1)
        q = q.reshape(B, L, ATTN_HEADS, 2, ATTN_HEAD_DIM)
        k = k.reshape(B, L, ATTN_HEADS, 2, ATTN_HEAD_DIM)
        v = v.reshape(B, L, ATTN_HEADS, 2 * ATTN_HEAD_DIM)
        lam = (jnp.exp(jnp.sum(lam_q1[l].astype(F32) * lam_k1[l].astype(F32)))
               - jnp.exp(jnp.sum(lam_q2[l].astype(F32) * lam_k2[l].astype(F32))) + lam_init)
        y_a = diff_attention(q, k, v, lam, lam_init, subln_g[l], rel_table)
        mix = jnp.concatenate([y_h, y_a], axis=-1) @ w_out[l]
        x = layer_norm(DEEPNORM_ALPHA * x + mix, ln1_g[l], ln1_b[l])
        moe = hier_moe(x, w_route_group[l], w_route_expert[l], w_gate[l], w_up[l], w_down[l])
        x = layer_norm(DEEPNORM_ALPHA * x + moe, ln2_g[l], ln2_b[l])
    return x


def setup_inputs(seed: int = 0) -> dict:
    key = jax.random.key(seed)
    ks = jax.random.split(key, 32)
    nrm = lambda k, s, sc: jax.random.normal(k, s, F32) * sc
    D = D_MODEL
    return {
        "x_prompt": nrm(ks[0], (BATCH, SEQ, D), 1.0),
        "x_sample": nrm(ks[1], (DEC_BATCH, DEC_SEQ, D), 1.0),
        "ln_emb_g": 1.0 + nrm(ks[2], (D,), 0.02),
        "ln_emb_b": nrm(ks[3], (D,), 0.02),
        "rel_table": nrm(ks[4], (REL_BUCKETS, ATTN_HEADS), 0.5),
        "w_in": nrm(ks[5], (DEPTH, D, D_IN), D ** -0.5),
        "conv_w": nrm(ks[6], (DEPTH, SHORT_CONV, 3 * D_HYENA), SHORT_CONV ** -0.5),
        "conv_b": nrm(ks[7], (DEPTH, 3 * D_HYENA), 0.02),
        "f_w1": nrm(ks[8], (DEPTH, FILTER_EMB, FILTER_WIDTH), FILTER_EMB ** -0.5),
        "f_b1": nrm(ks[9], (DEPTH, FILTER_WIDTH), 0.02),
        "f_freq": 1.0 + nrm(ks[10], (DEPTH, FILTER_WIDTH), 0.1),
        "f_w2": nrm(ks[11], (DEPTH, FILTER_WIDTH, FILTER_WIDTH), FILTER_WIDTH ** -0.5),
        "f_b2": nrm(ks[12], (DEPTH, FILTER_WIDTH), 0.02),
        "f_w3": nrm(ks[13], (DEPTH, FILTER_WIDTH, 2 * D_HYENA), FILTER_WIDTH ** -0.5),
        "h_bias": nrm(ks[14], (DEPTH, D_HYENA), 1.0),
        "lam_q1": nrm(ks[15], (DEPTH, ATTN_HEAD_DIM), 0.1),
        "lam_k1": nrm(ks[16], (DEPTH, ATTN_HEAD_DIM), 0.1),
        "lam_q2": nrm(ks[17], (DEPTH, ATTN_HEAD_DIM), 0.1),
        "lam_k2": nrm(ks[18], (DEPTH, ATTN_HEAD_DIM), 0.1),
        "subln_g": 1.0 + nrm(ks[19], (DEPTH, 2 * ATTN_HEAD_DIM), 0.02),
        "w_out": nrm(ks[20], (DEPTH, D_MIX, D), DEEPNORM_BETA * D_MIX ** -0.5),
        "ln1_g": 1.0 + nrm(ks[21], (DEPTH, D), 0.02),
        "ln1_b": nrm(ks[22], (DEPTH, D), 0.02),
        "w_route_group": nrm(ks[23], (DEPTH, D, N_GROUPS), D ** -0.5),
        "w_route_expert": nrm(ks[24], (DEPTH, D, N_EXPERTS), D ** -0.5),
        "w_gate": nrm(ks[25], (DEPTH, N_EXPERTS, D, D_EXPERT), D ** -0.5),
        "w_up": nrm(ks[26], (DEPTH, N_EXPERTS, D, D_EXPERT), D ** -0.5),
        "w_down": nrm(ks[27], (DEPTH, N_EXPERTS, D_EXPERT, D), DEEPNORM_BETA * D_EXPERT ** -0.5),
        "ln2_g": 1.0 + nrm(ks[28], (DEPTH, D), 0.02),
        "ln2_b": nrm(ks[29], (DEPTH, D), 0.02),
    }


def reference(x_prompt, x_sample, ln_emb_g, ln_emb_b, rel_table, w_in, conv_w, conv_b, f_w1, f_b1,
              f_freq, f_w2, f_b2, f_w3, h_bias, lam_q1, lam_k1, lam_q2, lam_k2, subln_g, w_out,
              ln1_g, ln1_b, w_route_group, w_route_expert, w_gate, w_up, w_down, ln2_g, ln2_b):
    y_prompt = trunk(x_prompt, ln_emb_g, ln_emb_b, rel_table, w_in, conv_w, conv_b, f_w1, f_b1, f_freq,
                     f_w2, f_b2, f_w3, h_bias, lam_q1, lam_k1, lam_q2, lam_k2, subln_g, w_out, ln1_g,
                     ln1_b, w_route_group, w_route_expert, w_gate, w_up, w_down, ln2_g, ln2_b)
    y_sample = trunk(x_sample, ln_emb_g, ln_emb_b, rel_table, w_in, conv_w, conv_b, f_w1, f_b1, f_freq,
                     f_w2, f_b2, f_w3, h_bias, lam_q1, lam_k1, lam_q2, lam_k2, subln_g, w_out, ln1_g,
                     ln1_b, w_route_group, w_route_expert, w_gate, w_up, w_down, ln2_g, ln2_b)
    return (y_prompt, y_sample)
```

```python
import functools
import math

import jax
import jax.numpy as jnp
from jax import lax
from jax.experimental import pallas as pl
from jax.experimental.pallas import tpu as pltpu

F32 = jnp.float32
BF16 = jnp.bfloat16

LN_EPS = 1e-5
REL_MAX_DIST = 128
TOP_K = 2
DECAY_TARGET = 1e-2
FAST_DECAY_PCT = 0.3
SLOW_DECAY_PCT = 1.5

LANES = 128
FFT_N2 = 128
VMEM_LIMIT_BYTES = 56 * 1024 * 1024


def _cparams(sem):
    return pltpu.CompilerParams(dimension_semantics=sem, vmem_limit_bytes=VMEM_LIMIT_BYTES)


def _dot(a, b):
    return jnp.dot(a, b, preferred_element_type=F32)


def _layer_norm_rows(z, g, b):
    mu = jnp.mean(z, axis=-1, keepdims=True)
    zc = z - mu
    var = jnp.mean(zc * zc, axis=-1, keepdims=True)
    return zc * lax.rsqrt(var + LN_EPS) * g + b


def _ln_kernel(xa_ref, xb_ref, g_ref, b_ref, o_ref, *, na):
    i = pl.program_id(0)

    @pl.when(i < na)
    def _():
        o_ref[...] = _layer_norm_rows(xa_ref[...], g_ref[...], b_ref[...])

    @pl.when(i >= na)
    def _():
        o_ref[...] = _layer_norm_rows(xb_ref[...], g_ref[...], b_ref[...])


def _ln_embed(xa, xb, g, b, tm):
    Ta, D = xa.shape
    Tb = xb.shape[0]
    na = Ta // tm
    nb = Tb // tm
    return pl.pallas_call(
        functools.partial(_ln_kernel, na=na),
        grid=(na + nb,),
        in_specs=[
            pl.BlockSpec((tm, D), lambda i: (jnp.minimum(i, na - 1), 0)),
            pl.BlockSpec((tm, D), lambda i: (jnp.maximum(i - na, 0), 0)),
            pl.BlockSpec((1, D), lambda i: (0, 0)),
            pl.BlockSpec((1, D), lambda i: (0, 0)),
        ],
        out_specs=pl.BlockSpec((tm, D), lambda i: (i, 0)),
        out_shape=jax.ShapeDtypeStruct((Ta + Tb, D), F32),
        compiler_params=_cparams(("arbitrary",)),
        name="ln_embed",
    )(xa, xb, g.reshape(1, D), b.reshape(1, D))


def _proj_in_kernel(x_ref, wh_ref, wk_ref, wqt_ref, wvt_ref, uh_ref, k_ref, qt_ref, vt_ref):
    xb = x_ref[...].astype(BF16)
    uh_ref[...] = _dot(xb, wh_ref[...]).astype(BF16)
    k_ref[...] = _dot(xb, wk_ref[...]).astype(BF16)
    nt = (((1,), (1,)), ((), ()))
    qt_ref[...] = lax.dot_general(wqt_ref[...], xb, nt, preferred_element_type=F32).astype(BF16)
    vt_ref[...] = lax.dot_general(wvt_ref[...], xb, nt, preferred_element_type=F32).astype(BF16)


def _proj_in(x2d, wh, wk, wqt, wvt, B, L, tm):
    T, D = x2d.shape
    Ch = wh.shape[1]
    Da = wk.shape[1]
    nl = L // tm
    return pl.pallas_call(
        _proj_in_kernel,
        grid=(B, nl),
        in_specs=[
            pl.BlockSpec((tm, D), lambda b, i: (b * nl + i, 0)),
            pl.BlockSpec((D, Ch), lambda b, i: (0, 0)),
            pl.BlockSpec((D, Da), lambda b, i: (0, 0)),
            pl.BlockSpec((Da, D), lambda b, i: (0, 0)),
            pl.BlockSpec((Da, D), lambda b, i: (0, 0)),
        ],
        out_specs=[
            pl.BlockSpec((tm, Ch), lambda b, i: (b * nl + i, 0)),
            pl.BlockSpec((tm, Da), lambda b, i: (b * nl + i, 0)),
            pl.BlockSpec((None, Da, tm), lambda b, i: (b, 0, i)),
            pl.BlockSpec((None, Da, tm), lambda b, i: (b, 0, i)),
        ],
        out_shape=[
            jax.ShapeDtypeStruct((T, Ch), BF16),
            jax.ShapeDtypeStruct((T, Da), BF16),
            jax.ShapeDtypeStruct((B, Da, L), BF16),
            jax.ShapeDtypeStruct((B, Da, L), BF16),
        ],
        compiler_params=_cparams(("parallel", "parallel")),
        name="proj_in",
    )(x2d, wh, wk, wqt, wvt)


def _hyena_pre_kernel(u_ref, up_ref, un_ref, w_ref, b_ref, vx_ref, x0_ref, *, dh, halo):
    i = pl.program_id(1)
    last = pl.num_programs(1) - 1
    u = u_ref[...].astype(F32)
    tm = u.shape[0]
    prev_row = up_ref[halo - 1:halo, :].astype(F32)
    next_row = un_ref[0:1, :].astype(F32)
    prev_row = jnp.where(i == 0, 0.0, prev_row)
    next_row = jnp.where(i == last, 0.0, next_row)
    rows = lax.broadcasted_iota(jnp.int32, (tm, 1), 0)
    um1 = jnp.where(rows == 0, prev_row, pltpu.roll(u, 1, axis=0))
    up1 = jnp.where(rows == tm - 1, next_row, pltpu.roll(u, tm - 1, axis=0))
    w = w_ref[...]
    y = b_ref[...] + um1 * w[0:1, :]
    y = y + u * w[1:2, :]
    y = y + up1 * w[2:3, :]
    x0_ref[...] = y[:, :dh].astype(BF16)
    vx_ref[...] = y[:, 2 * dh:] * y[:, dh:2 * dh]


def _hyena_pre(uh, conv_w, conv_b, B, L, tm):
    T, Ch = uh.shape
    dh = Ch // 3
    halo = 16
    nl = L // tm
    r = tm // halo
    nhalo = T // halo
    return pl.pallas_call(
        functools.partial(_hyena_pre_kernel, dh=dh, halo=halo),
        grid=(B, nl),
        in_specs=[
            pl.BlockSpec((tm, Ch), lambda b, i: (b * nl + i, 0)),
            pl.BlockSpec((halo, Ch), lambda b, i: (jnp.maximum((b * nl + i) * r - 1, 0), 0)),
            pl.BlockSpec((halo, Ch), lambda b, i: (jnp.minimum((b * nl + i + 1) * r, nhalo - 1), 0)),
            pl.BlockSpec((3, Ch), lambda b, i: (0, 0)),
            pl.BlockSpec((1, Ch), lambda b, i: (0, 0)),
        ],
        out_specs=[
            pl.BlockSpec((tm, dh), lambda b, i: (b * nl + i, 0)),
            pl.BlockSpec((tm, dh), lambda b, i: (b * nl + i, 0)),
        ],
        out_shape=[
            jax.ShapeDtypeStruct((T, dh), F32),
            jax.ShapeDtypeStruct((T, dh), BF16),
        ],
        compiler_params=_cparams(("parallel", "parallel")),
        name="hyena_pre",
    )(uh, uh, uh, conv_w, conv_b.reshape(1, Ch))


def _dft_tables(L):
    N = 2 * L
    N1 = N // FFT_N2
    NB = L // FFT_N2
    n2 = jnp.arange(FFT_N2, dtype=jnp.int32)[:, None, None]
    k1 = jnp.arange(N1, dtype=jnp.int32)[None, :, None]
    n1 = jnp.arange(N1, dtype=jnp.int32)[None, None, :]
    ph = (k1 * (FFT_N2 * n1 + n2)) % N
    ang = ph.astype(F32) * (2.0 * math.pi / N)
    c, s = jnp.cos(ang), jnp.sin(ang)
    g_full = jnp.concatenate([c, -s], axis=1).astype(BF16)
    g_fwd = g_full[:, :, :NB]
    ci = jnp.transpose(c[:, :, :NB], (0, 2, 1))
    si = jnp.transpose(s[:, :, :NB], (0, 2, 1))
    g_inv = jnp.concatenate([ci, -si], axis=2).astype(BF16)
    a = jnp.arange(FFT_N2, dtype=jnp.int32)
    ang2 = ((a[:, None] * a[None, :]) % FFT_N2).astype(F32) * (2.0 * math.pi / FFT_N2)
    c2, s2 = jnp.cos(ang2), jnp.sin(ang2)
    f2 = jnp.concatenate([jnp.concatenate([c2, s2], 1), jnp.concatenate([-s2, c2], 1)], 0).astype(BF16)
    f2i = jnp.concatenate([jnp.concatenate([c2, -s2], 1), jnp.concatenate([s2, c2], 1)], 0).astype(BF16)
    return g_full, g_fwd, g_inv, f2, f2i


def _filter_spec_kernel(kf_ref, g_ref, f2_ref, h_ref, s_ref, *, n1, inv_n):
    two_n1 = 2 * n1

    def stage_a(n2, carry):
        xs = kf_ref[pl.ds(n2, n1, stride=FFT_N2), :]
        row = pl.multiple_of(n2 * two_n1, two_n1)
        s_ref[pl.ds(row, two_n1), :] = _dot(g_ref[n2], xs.astype(BF16))
        return carry

    lax.fori_loop(0, FFT_N2, stage_a, 0)

    def stage_b(k1, carry):
        zr = s_ref[pl.ds(k1, FFT_N2, stride=two_n1), :]
        zi = s_ref[pl.ds(n1 + k1, FFT_N2, stride=two_n1), :]
        z = jnp.concatenate([zr, zi], axis=0).astype(BF16)
        h_ref[k1] = (_dot(f2_ref[...], z) * inv_n).astype(BF16)
        return carry

    lax.fori_loop(0, n1, stage_b, 0)


def _filter_spec(k_full, g_full, f2, cc):
    N, Dh = k_full.shape
    n1 = N // FFT_N2
    once = pl.Buffered(1)
    return pl.pallas_call(
        functools.partial(_filter_spec_kernel, n1=n1, inv_n=1.0 / N),
        grid=(Dh // cc,),
        in_specs=[
            pl.BlockSpec((N, cc), lambda c: (0, c), pipeline_mode=once),
            pl.BlockSpec((FFT_N2, 2 * n1, n1), lambda c: (0, 0, 0), pipeline_mode=once),
            pl.BlockSpec((2 * FFT_N2, 2 * FFT_N2), lambda c: (0, 0)),
        ],
        out_specs=pl.BlockSpec((n1, 2 * FFT_N2, cc), lambda c: (0, 0, c)),
        out_shape=jax.ShapeDtypeStruct((n1, 2 * FFT_N2, Dh), BF16),
        scratch_shapes=[pltpu.VMEM((FFT_N2 * 2 * n1, cc), F32)],
        compiler_params=_cparams(("arbitrary",)),
        name="filter_spec",
    )(k_full, g_full, f2)


def _fft_conv_kernel(vx_ref, gf_ref, gi_ref, f2_ref, f2i_ref, h_ref, o_ref, s_ref, *, n1, nb):
    two_n1 = 2 * n1

    def stage_a(n2, carry):
        xs = vx_ref[pl.ds(n2, nb, stride=FFT_N2), :]
        row = pl.multiple_of(n2 * two_n1, two_n1)
        s_ref[pl.ds(row, two_n1), :] = _dot(gf_ref[n2], xs.astype(BF16))
        return carry

    lax.fori_loop(0, FFT_N2, stage_a, 0)

    def stage_b(k1, carry):
        zr = s_ref[pl.ds(k1, FFT_N2, stride=two_n1), :]
        zi = s_ref[pl.ds(n1 + k1, FFT_N2, stride=two_n1), :]
        z = jnp.concatenate([zr, zi], axis=0).astype(BF16)
        sp = _dot(f2_ref[...], z)
        h = h_ref[k1].astype(F32)
        sr, si = sp[:FFT_N2], sp[FFT_N2:]
        hr, hi = h[:FFT_N2], h[FFT_N2:]
        y = jnp.concatenate([sr * hr - si * hi, sr * hi + si * hr], axis=0).astype(BF16)
        bk = _dot(f2i_ref[...], y)
        s_ref[pl.ds(k1, FFT_N2, stride=two_n1), :] = bk[:FFT_N2]
        s_ref[pl.ds(n1 + k1, FFT_N2, stride=two_n1), :] = bk[FFT_N2:]
        return carry

    lax.fori_loop(0, n1, stage_b, 0)

    def stage_c(n2, carry):
        row = pl.multiple_of(n2 * two_n1, two_n1)
        bb = s_ref[pl.ds(row, two_n1), :].astype(BF16)
        o_ref[pl.ds(n2, nb, stride=FFT_N2), :] = _dot(gi_ref[n2], bb)
        return carry

    lax.fori_loop(0, FFT_N2, stage_c, 0)


def _fft_conv(vx, g_fwd, g_inv, f2, f2i, h_spec, B, L, cc):
    T, Dh = vx.shape
    n1 = 2 * L // FFT_N2
    nb = L // FFT_N2
    once = pl.Buffered(1)
    return pl.pallas_call(
        functools.partial(_fft_conv_kernel, n1=n1, nb=nb),
        grid=(Dh // cc, B),
        in_specs=[
            pl.BlockSpec((L, cc), lambda c, b: (b, c)),
            pl.BlockSpec((FFT_N2, 2 * n1, nb), lambda c, b: (0, 0, 0), pipeline_mode=once),
            pl.BlockSpec((FFT_N2, nb, 2 * n1), lambda c, b: (0, 0, 0), pipeline_mode=once),
            pl.BlockSpec((2 * FFT_N2, 2 * FFT_N2), lambda c, b: (0, 0)),
            pl.BlockSpec((2 * FFT_N2, 2 * FFT_N2), lambda c, b: (0, 0)),
            pl.BlockSpec((n1, 2 * FFT_N2, cc), lambda c, b: (0, 0, c), pipeline_mode=once),
        ],
        out_specs=pl.BlockSpec((L, cc), lambda c, b: (b, c)),
        out_shape=jax.ShapeDtypeStruct((T, Dh), F32),
        scratch_shapes=[pltpu.VMEM((FFT_N2 * 2 * n1, cc), F32)],
        compiler_params=_cparams(("arbitrary", "arbitrary")),
        name="fft_conv",
    )(vx, g_fwd, g_inv, f2, f2i, h_spec)


def _bias_tiles_kernel(table_ref, o_ref, *, t, n_buckets):
    h = pl.program_id(0)
    c = pl.program_id(1)
    nb = n_buckets // 2
    max_exact = nb // 2
    kv = lax.broadcasted_iota(jnp.int32, (t, t), 0)
    q = lax.broadcasted_iota(jnp.int32, (t, t), 1)
    rel = (c - 2) * t + kv - q
    ret = jnp.where(rel > 0, nb, 0)
    n = jnp.abs(rel)
    large = max_exact + (jnp.log(jnp.maximum(n, 1).astype(F32) / max_exact)
                         / math.log(REL_MAX_DIST / max_exact) * (nb - max_exact)).astype(jnp.int32)
    large = jnp.minimum(large, nb - 1)
    bucket = ret + jnp.where(n < max_exact, n, large)
    out = jnp.zeros((t, t), F32)
    for b in range(n_buckets):
        out = jnp.where(bucket == b, table_ref[b, h], out)
    o_ref[...] = out


def _bias_tiles(rel_table, t):
    n_buckets, H = rel_table.shape
    return pl.pallas_call(
        functools.partial(_bias_tiles_kernel, t=t, n_buckets=n_buckets),
        grid=(H, 5),
        in_specs=[pl.BlockSpec(memory_space=pltpu.SMEM)],
        out_specs=pl.BlockSpec((None, None, t, t), lambda h, c: (h, c, 0, 0)),
        out_shape=jax.ShapeDtypeStruct((H, 5, t, t), F32),
        compiler_params=_cparams(("parallel", "parallel")),
        name="bias_tiles",
    )(rel_table)


def _attn_kernel(lam_ref, qt_ref, k_ref, vt_ref, bias_ref, g_ref, o_ref, acc1_ref, acc2_ref,
                 *, t, nk, hd, out_scale):
    qi = pl.program_id(2)
    qt = qt_ref[...]
    half = lax.broadcasted_iota(jnp.int32, qt.shape, 0) < hd
    zero = jnp.zeros_like(qt)
    q1 = jnp.where(half, qt, zero)
    q2 = jnp.where(half, zero, qt)
    acc1_ref[...] = jnp.zeros_like(acc1_ref)
    acc2_ref[...] = jnp.zeros_like(acc2_ref)

    def online(s, m, l, acc_ref, vb):
        m_new = jnp.maximum(m, jnp.max(s, axis=0, keepdims=True))
        alpha = jnp.exp(m - m_new)
        p = jnp.exp(s - m_new)
        l_new = alpha * l + jnp.sum(p, axis=0, keepdims=True)
        acc_ref[...] = alpha * acc_ref[...] + _dot(vb, p.astype(BF16))
        return m_new, l_new

    def step(ki, carry):
        m1, l1, m2, l2 = carry
        off = pl.multiple_of(ki * t, t)
        kb = k_ref[pl.ds(off, t), :]
        vb = vt_ref[:, pl.ds(off, t)]
        bias = bias_ref[jnp.clip(ki - qi, -2, 2) + 2]
        s1 = _dot(kb, q1) + bias
        s2 = _dot(kb, q2) + bias
        m1, l1 = online(s1, m1, l1, acc1_ref, vb)
        m2, l2 = online(s2, m2, l2, acc2_ref, vb)
        return m1, l1, m2, l2

    neg = jnp.full((1, t), -jnp.inf, F32)
    zer = jnp.zeros((1, t), F32)
    m1, l1, m2, l2 = lax.fori_loop(0, nk, step, (neg, zer, neg, zer))
    o = acc1_ref[...] / l1 - lam_ref[0] * (acc2_ref[...] / l2)
    ms = jnp.mean(o * o, axis=0, keepdims=True)
    o = o * lax.rsqrt(ms + LN_EPS) * g_ref[...] * out_scale
    o_ref[...] = o.T.astype(BF16)


def _attention(lam, qt, k, vt, bias, g_col, B, L, H, hd, t, lam_init):
    Da = H * 2 * hd
    nq = L // t
    k3 = k.reshape(B, L, Da)
    out = pl.pallas_call(
        functools.partial(_attn_kernel, t=t, nk=nq, hd=hd, out_scale=1.0 - lam_init),
        grid=(B, H, nq),
        in_specs=[
            pl.BlockSpec(memory_space=pltpu.SMEM),
            pl.BlockSpec((None, 2 * hd, t), lambda b, h, q: (b, h, q)),
            pl.BlockSpec((None, L, 2 * hd), lambda b, h, q: (b, 0, h)),
            pl.BlockSpec((None, 2 * hd, L), lambda b, h, q: (b, h, 0)),
            pl.BlockSpec((None, 5, t, t), lambda b, h, q: (h, 0, 0, 0)),
            pl.BlockSpec((2 * hd, 1), lambda b, h, q: (0, 0)),
        ],
        out_specs=pl.BlockSpec((None, t, 2 * hd), lambda b, h, q: (b, q, h)),
        out_shape=jax.ShapeDtypeStruct((B, L, Da), BF16),
        scratch_shapes=[pltpu.VMEM((2 * hd, t), F32), pltpu.VMEM((2 * hd, t), F32)],
        compiler_params=_cparams(("parallel", "parallel", "arbitrary")),
        name="diff_attn",
    )(lam, qt, k3, vt, bias, g_col)
    return out.reshape(B * L, Da)


def _proj_out_kernel(cy_ref, vx_ref, x0_ref, hb_ref, ya_ref, x_ref, woh_ref, woa_ref, g_ref, b_ref,
                     wrc_ref, wrh_ref, x1_ref, eid_ref, gate_ref, *, alpha, n_groups, per_group):
    yh = ((cy_ref[...] + vx_ref[...] * hb_ref[...]) * x0_ref[...].astype(F32)).astype(BF16)
    mix = _dot(yh, woh_ref[...]) + _dot(ya_ref[...], woa_ref[...])
    x1 = _layer_norm_rows(alpha * x_ref[...] + mix, g_ref[...], b_ref[...])
    x1_ref[...] = x1
    hi = x1.astype(BF16)
    lo = (x1 - hi.astype(F32)).astype(BF16)
    lg2 = _dot(hi, wrc_ref[...])
    lg = lg2[:, :LANES] + lg2[:, LANES:] + _dot(lo, wrh_ref[...])
    col = lax.broadcasted_iota(jnp.int32, lg.shape, 1).astype(F32)
    far = jnp.float32(LANES)
    neg_inf = jnp.float32(-jnp.inf)

    def first_col(mask):
        return jnp.min(jnp.where(mask, col, far), axis=-1, keepdims=True)

    gmask = col < n_groups
    glog = jnp.where(gmask, lg, neg_inf)
    gmax = jnp.max(glog, axis=-1, keepdims=True)
    g_sel = first_col(glog == gmax)
    g_w = 1.0 / jnp.sum(jnp.where(gmask, jnp.exp(glog - gmax), 0.0), axis=-1, keepdims=True)
    lo_col = n_groups + g_sel * per_group
    emask = (col >= lo_col) & (col < lo_col + per_group)
    elog = jnp.where(emask, lg, neg_inf)
    emax = jnp.max(elog, axis=-1, keepdims=True)
    eexp = jnp.where(emask, jnp.exp(elog - emax), 0.0)
    prob = eexp / jnp.sum(eexp, axis=-1, keepdims=True)
    p1 = jnp.max(prob, axis=-1, keepdims=True)
    i1 = first_col(emask & (prob == p1))
    mask2 = emask & (col != i1)
    p2 = jnp.max(jnp.where(mask2, prob, -1.0), axis=-1, keepdims=True)
    i2 = first_col(mask2 & (prob == p2))
    psum = p1 + p2
    eid = jnp.where(col == 0, i1 - n_groups, jnp.where(col == 1, i2 - n_groups, 0.0))
    eid_ref[...] = eid.astype(jnp.int32)
    gate_ref[...] = jnp.where(col == 0, g_w * p1 / psum, jnp.where(col == 1, g_w * p2 / psum, 0.0))


def _proj_out(cy, vx, x0c, hb, ya, x, woh, woa, g, b, wrc, wrh, alpha, n_groups, per_group, tm):
    T, D = x.shape
    Dh = cy.shape[1]
    Da = ya.shape[1]
    row = lambda i: (i, 0)
    fix = lambda i: (0, 0)
    return pl.pallas_call(
        functools.partial(_proj_out_kernel, alpha=alpha, n_groups=n_groups, per_group=per_group),
        grid=(T // tm,),
        in_specs=[
            pl.BlockSpec((tm, Dh), row),
            pl.BlockSpec((tm, Dh), row),
            pl.BlockSpec((tm, Dh), row),
            pl.BlockSpec((1, Dh), fix),
            pl.BlockSpec((tm, Da), row),
            pl.BlockSpec((tm, D), row),
            pl.BlockSpec((Dh, D), fix),
            pl.BlockSpec((Da, D), fix),
            pl.BlockSpec((1, D), fix),
            pl.BlockSpec((1, D), fix),
            pl.BlockSpec((D, 2 * LANES), fix),
            pl.BlockSpec((D, LANES), fix),
        ],
        out_specs=[
            pl.BlockSpec((tm, D), row),
            pl.BlockSpec((tm, LANES), row),
            pl.BlockSpec((tm, LANES), row),
        ],
        out_shape=[
            jax.ShapeDtypeStruct((T, D), F32),
            jax.ShapeDtypeStruct((T, LANES), jnp.int32),
            jax.ShapeDtypeStruct((T, LANES), F32),
        ],
        compiler_params=_cparams(("parallel",)),
        name="proj_out_ln_router",
    )(cy, vx, x0c, hb, ya, x, woh, woa, g, b, wrc, wrh)


def _moe_kernel(be_ref, nu_ref, src_ref, dst_ref, x_hbm, wg_ref, wu_ref, wd_ref, out_hbm,
                xbuf, ybuf, wgb, wub, wdb, sem_in, sem_out, *, bm):
    i = pl.program_id(0)

    def row_in(r, tok):
        return pltpu.make_async_copy(x_hbm.at[pl.ds(tok, 1), :], xbuf.at[pl.ds(r, 1), :], sem_in)

    def row_out(r, d):
        return pltpu.make_async_copy(ybuf.at[pl.ds(r, 1), :], out_hbm.at[pl.ds(d, 1), :], sem_out)

    @pl.when(i < nu_ref[0])
    def _():
        def issue_in(r, c):
            row_in(r, src_ref[0, r]).start()
            return c

        lax.fori_loop(0, bm, issue_in, 0)

        changed = jnp.logical_or(i == 0, be_ref[i] != be_ref[jnp.maximum(i - 1, 0)])

        @pl.when(changed)
        def _():
            wgb[...] = wg_ref[...].astype(BF16)
            wub[...] = wu_ref[...].astype(BF16)
            wdb[...] = wd_ref[...].astype(BF16)

        def wait_in(r, c):
            row_in(r, 0).wait()
            return c

        lax.fori_loop(0, bm, wait_in, 0)

        xb = xbuf[...].astype(BF16)
        hg = _dot(xb, wgb[...])
        hu = _dot(xb, wub[...])
        act = (hg / (1.0 + jnp.exp(-hg))) * hu
        ybuf[...] = _dot(act.astype(BF16), wdb[...])

        def issue_out(r, c):
            d = dst_ref[0, r]

            @pl.when(d >= 0)
            def _():
                row_out(r, d).start()

            return c

        lax.fori_loop(0, bm, issue_out, 0)

        def wait_out(r, c):
            d = dst_ref[0, r]

            @pl.when(d >= 0)
            def _():
                row_out(r, 0).wait()

            return c

        lax.fori_loop(0, bm, wait_out, 0)


def _moe_experts(block_e, n_used, src, dst, x1, w_gate, w_up, w_down, n_rows_out, bm):
    n_blocks = block_e.shape[0]
    T, D = x1.shape
    E, _, De = w_gate.shape
    grid_spec = pltpu.PrefetchScalarGridSpec(
        num_scalar_prefetch=2,
        grid=(n_blocks,),
        in_specs=[
            pl.BlockSpec((None, 1, bm), lambda i, be, nu: (i, 0, 0), memory_space=pltpu.SMEM),
            pl.BlockSpec((None, 1, bm), lambda i, be, nu: (i, 0, 0), memory_space=pltpu.SMEM),
            pl.BlockSpec(memory_space=pl.ANY),
            pl.BlockSpec((None, D, De), lambda i, be, nu: (be[i], 0, 0)),
            pl.BlockSpec((None, D, De), lambda i, be, nu: (be[i], 0, 0)),
            pl.BlockSpec((None, De, D), lambda i, be, nu: (be[i], 0, 0)),
        ],
        out_specs=pl.BlockSpec(memory_space=pl.ANY),
        scratch_shapes=[
            pltpu.VMEM((bm, D), F32),
            pltpu.VMEM((bm, D), F32),
            pltpu.VMEM((D, De), BF16),
            pltpu.VMEM((D, De), BF16),
            pltpu.VMEM((De, D), BF16),
            pltpu.SemaphoreType.DMA(()),
            pltpu.SemaphoreType.DMA(()),
        ],
    )
    return pl.pallas_call(
        functools.partial(_moe_kernel, bm=bm),
        grid_spec=grid_spec,
        out_shape=jax.ShapeDtypeStruct((n_rows_out, D), F32),
        compiler_params=_cparams(("arbitrary",)),
        name="moe_experts",
    )(block_e, n_used, src, dst, x1, w_gate, w_up, w_down)


def _combine_kernel(y2_ref, gate_ref, x1_ref, g_ref, b_ref, o_ref, *, alpha, d):
    gates = gate_ref[...]
    y = gates[:, 0:1] * y2_ref[:, :d] + gates[:, 1:2] * y2_ref[:, d:]
    o_ref[...] = _layer_norm_rows(alpha * x1_ref[...] + y, g_ref[...], b_ref[...])


def _combine_ln(y2, gates, x1, g, b, alpha, tm):
    T, D = x1.shape
    row = lambda i: (i, 0)
    fix = lambda i: (0, 0)
    return pl.pallas_call(
        functools.partial(_combine_kernel, alpha=alpha, d=D),
        grid=(T // tm,),
        in_specs=[
            pl.BlockSpec((tm, TOP_K * D), row),
            pl.BlockSpec((tm, LANES), row),
            pl.BlockSpec((tm, D), row),
            pl.BlockSpec((1, D), fix),
            pl.BlockSpec((1, D), fix),
        ],
        out_specs=pl.BlockSpec((tm, D), row),
        out_shape=jax.ShapeDtypeStruct((T, D), F32),
        compiler_params=_cparams(("parallel",)),
        name="moe_combine_ln",
    )(y2, gates, x1, g, b)


def _hyena_filter(L, dh, f_w1, f_b1, f_freq, f_w2, f_b2, f_w3):
    emb = f_w1.shape[0]
    bands = (emb - 1) // 2
    t = jnp.linspace(0.0, 1.0, L, dtype=F32)[:, None]
    w = (2.0 * math.pi / L) * jnp.arange(L, dtype=F32)[:, None]
    f = jnp.linspace(1e-4, bands - 1, bands, dtype=F32)[None, :]
    z = jnp.concatenate([t, jnp.cos(f * w), -jnp.sin(f * w)], axis=-1)
    hp = lax.Precision.HIGHEST
    h = jnp.sin(f_freq * (jnp.dot(z, f_w1, precision=hp) + f_b1))
    h = jnp.sin(f_freq * (jnp.dot(h, f_w2, precision=hp) + f_b2))
    h = jnp.dot(h, f_w3, precision=hp)
    max_decay = math.log(DECAY_TARGET) / FAST_DECAY_PCT
    min_decay = math.log(DECAY_TARGET) / SLOW_DECAY_PCT
    deltas = jnp.abs(jnp.linspace(min_decay, max_decay, dh, dtype=F32))
    decay = jnp.exp(-t * deltas)
    h_fwd = h[:, :dh] * decay
    h_bwd = h[:, dh:] * decay
    k_full = jnp.concatenate([h_fwd, jnp.zeros((1, dh), F32), h_bwd[1:][::-1]], axis=0)
    return k_full / jnp.sum(jnp.abs(k_full), axis=0, keepdims=True)


def _dispatch_tables(eid, n_experts, bm):
    T = eid.shape[0]
    A = T * TOP_K
    flat_e = eid.reshape(A)
    order = jnp.argsort(flat_e, stable=True).astype(jnp.int32)
    se = flat_e[order]
    counts = jnp.zeros((n_experts,), jnp.int32).at[flat_e].add(1)
    starts = jnp.cumsum(counts) - counts
    pcounts = (counts + bm - 1) // bm * bm
    pends = jnp.cumsum(pcounts)
    pstarts = pends - pcounts
    dest = pstarts[se] + (jnp.arange(A, dtype=jnp.int32) - starts[se])
    n_blocks = -(-A // bm) + n_experts
    P = n_blocks * bm
    row_asg = jnp.full((P,), -1, jnp.int32).at[dest].set(order)
    src = jnp.maximum(row_asg, 0) // TOP_K
    block_e = jnp.minimum(
        jnp.searchsorted(pends, jnp.arange(n_blocks, dtype=jnp.int32) * bm, side="right"),
        n_experts - 1).astype(jnp.int32)
    n_used = (pends[-1] // bm).astype(jnp.int32).reshape(1)
    return block_e, n_used, src.reshape(n_blocks, 1, bm), row_asg.reshape(n_blocks, 1, bm)


def _pick(n, pref):
    t = min(n, pref)
    assert n % t == 0, (n, t)
    return t


def _trunk(xa, xb, p):
    Ba, L, D = xa.shape
    Bb = xb.shape[0]
    assert xb.shape[1:] == (L, D)
    B = Ba + Bb
    T = B * L
    depth = p["w_in"].shape[0]
    dh = p["h_bias"].shape[1]
    H = p["rel_table"].shape[1]
    hd = p["lam_q1"].shape[1]
    da = H * 2 * hd
    n_groups = p["w_route_group"].shape[2]
    n_experts = p["w_route_expert"].shape[2]
    per_group = n_experts // n_groups
    alpha = (2.0 * depth) ** 0.25
    assert L % FFT_N2 == 0 and (L // FFT_N2) % 8 == 0
    assert 2 * hd == LANES and dh % LANES == 0 and n_groups + n_experts <= LANES

    tm = _pick(L, 512)
    t_attn = _pick(L, 512)
    bm = 512

    xs = _ln_embed(xa.reshape(Ba * L, D), xb.reshape(Bb * L, D), p["ln_emb_g"], p["ln_emb_b"], tm)
    tables = _dft_tables(L)
    g_full, g_fwd, g_inv, f2, f2i = tables
    bias = _bias_tiles(p["rel_table"], t_attn)

    for l in range(depth):
        lam_init = 0.8 - 0.6 * math.exp(-0.3 * l)
        w_in = p["w_in"][l]
        wh = w_in[:, :3 * dh].astype(BF16)
        wa = w_in[:, 3 * dh:]
        wqt = (wa[:, :da] * (hd ** -0.5)).T.astype(BF16)
        wk = wa[:, da:2 * da].astype(BF16)
        wvt = wa[:, 2 * da:].T.astype(BF16)
        uh, k, qt, vt = _proj_in(xs, wh, wk, wqt, wvt, B, L, tm)

        vx, x0c = _hyena_pre(uh, p["conv_w"][l], p["conv_b"][l], B, L, tm)
        k_full = _hyena_filter(L, dh, p["f_w1"][l], p["f_b1"][l], p["f_freq"][l], p["f_w2"][l],
                               p["f_b2"][l], p["f_w3"][l])
        h_spec = _filter_spec(k_full, g_full, f2, LANES)
        cy = _fft_conv(vx, g_fwd, g_inv, f2, f2i, h_spec, B, L, LANES)

        lam = (jnp.exp(jnp.sum(p["lam_q1"][l] * p["lam_k1"][l]))
               - jnp.exp(jnp.sum(p["lam_q2"][l] * p["lam_k2"][l])) + lam_init).reshape(1).astype(F32)
        ya = _attention(lam, qt, k, vt, bias, p["subln_g"][l].reshape(2 * hd, 1), B, L, H, hd, t_attn,
                        lam_init)

        w_out = p["w_out"][l]
        w_r = jnp.concatenate([p["w_route_group"][l], p["w_route_expert"][l]], axis=1)
        w_r = jnp.pad(w_r, ((0, 0), (0, LANES - w_r.shape[1])))
        w_r_hi = w_r.astype(BF16)
        w_r_lo = (w_r - w_r_hi.astype(F32)).astype(BF16)
        x1, eid, gates = _proj_out(
            cy, vx, x0c, p["h_bias"][l].reshape(1, dh), ya, xs,
            w_out[:dh].astype(BF16), w_out[dh:].astype(BF16),
            p["ln1_g"][l].reshape(1, D), p["ln1_b"][l].reshape(1, D),
            jnp.concatenate([w_r_hi, w_r_lo], axis=1), w_r_hi,
            alpha, n_groups, per_group, tm)

        block_e, n_used, src, dst = _dispatch_tables(eid[:, :TOP_K], n_experts, bm)
        y2 = _moe_experts(block_e, n_used, src, dst, x1, p["w_gate"][l], p["w_up"][l], p["w_down"][l],
                          T * TOP_K, bm)
        xs = _combine_ln(y2.reshape(T, TOP_K * D), gates, x1, p["ln2_g"][l].reshape(1, D),
                         p["ln2_b"][l].reshape(1, D), alpha, tm)
    return xs[:Ba * L].reshape(Ba, L, D), xs[Ba * L:].reshape(Bb, L, D)


def kernel(x_prompt, x_sample, ln_emb_g, ln_emb_b, rel_table, w_in, conv_w, conv_b, f_w1, f_b1, f_freq,
           f_w2, f_b2, f_w3, h_bias, lam_q1, lam_k1, lam_q2, lam_k2, subln_g, w_out, ln1_g, ln1_b,
           w_route_group, w_route_expert, w_gate, w_up, w_down, ln2_g, ln2_b):
    p = dict(ln_emb_g=ln_emb_g, ln_emb_b=ln_emb_b, rel_table=rel_table, w_in=w_in, conv_w=conv_w,
             conv_b=conv_b, f_w1=f_w1, f_b1=f_b1, f_freq=f_freq, f_w2=f_w2, f_b2=f_b2, f_w3=f_w3,
             h_bias=h_bias, lam_q1=lam_q1, lam_k1=lam_k1, lam_q2=lam_q2, lam_k2=lam_k2,
             subln_g=subln_g, w_out=w_out, ln1_g=ln1_g, ln1_b=ln1_b, w_route_group=w_route_group,
             w_route_expert=w_route_expert, w_gate=w_gate, w_up=w_up, w_down=w_down, ln2_g=ln2_g,
             ln2_b=ln2_b)
    return _trunk(x_prompt, x_sample, p)
```

```python
import functools
import math

import jax
import jax.numpy as jnp
from jax import lax
from jax.experimental import pallas as pl
from jax.experimental.pallas import tpu as pltpu

F32 = jnp.float32
BF16 = jnp.bfloat16

LN_EPS = 1e-5
REL_MAX_DIST = 128
TOP_K = 2
DECAY_TARGET = 1e-2
FAST_DECAY_PCT = 0.3
SLOW_DECAY_PCT = 1.5

LANES = 128
BF16_SUBLANES = 16
LOG2E = 1.4426950408889634
FFT_N2 = 128
VMEM_LIMIT_BYTES = 56 * 1024 * 1024


def _cparams(sem):
    return pltpu.CompilerParams(dimension_semantics=sem, vmem_limit_bytes=VMEM_LIMIT_BYTES)


def _dot(a, b):
    return jnp.dot(a, b, preferred_element_type=F32)


def _layer_norm_rows(z, g, b):
    mu = jnp.mean(z, axis=-1, keepdims=True)
    zc = z - mu
    var = jnp.mean(zc * zc, axis=-1, keepdims=True)
    return zc * lax.rsqrt(var + LN_EPS) * g + b


def _ln_kernel(xa_ref, xb_ref, g_ref, b_ref, o_ref, *, na):
    i = pl.program_id(0)

    @pl.when(i < na)
    def _():
        o_ref[...] = _layer_norm_rows(xa_ref[...], g_ref[...], b_ref[...])

    @pl.when(i >= na)
    def _():
        o_ref[...] = _layer_norm_rows(xb_ref[...], g_ref[...], b_ref[...])


def _ln_embed(xa, xb, g, b, tm):
    Ta, D = xa.shape
    Tb = xb.shape[0]
    na = Ta // tm
    nb = Tb // tm
    return pl.pallas_call(
        functools.partial(_ln_kernel, na=na),
        grid=(na + nb,),
        in_specs=[
            pl.BlockSpec((tm, D), lambda i: (jnp.minimum(i, na - 1), 0)),
            pl.BlockSpec((tm, D), lambda i: (jnp.maximum(i - na, 0), 0)),
            pl.BlockSpec((1, D), lambda i: (0, 0)),
            pl.BlockSpec((1, D), lambda i: (0, 0)),
        ],
        out_specs=pl.BlockSpec((tm, D), lambda i: (i, 0)),
        out_shape=jax.ShapeDtypeStruct((Ta + Tb, D), F32),
        compiler_params=_cparams(("arbitrary",)),
        name="ln_embed",
    )(xa, xb, g.reshape(1, D), b.reshape(1, D))


def _proj_in_kernel(x_ref, wh_ref, wk_ref, wqt_ref, wvt_ref, uh_ref, k_ref, qt_ref, vt_ref):
    xb = x_ref[...].astype(BF16)
    uh_ref[...] = _dot(xb, wh_ref[...]).astype(BF16)
    k_ref[...] = _dot(xb, wk_ref[...]).astype(BF16)
    nt = (((1,), (1,)), ((), ()))
    qt_ref[...] = lax.dot_general(wqt_ref[...], xb, nt, preferred_element_type=F32).astype(BF16)
    vt = lax.dot_general(wvt_ref[...], xb, nt, preferred_element_type=F32).astype(BF16)
    n_heads, dvp, tm = vt_ref.shape
    dv = vt.shape[0] // n_heads
    pad_rows = lax.broadcasted_iota(jnp.int32, (dvp - dv, tm), 0)
    ones_pad = jnp.where(pad_rows == 0, 1.0, 0.0).astype(BF16)
    for h in range(n_heads):
        vt_ref[h, 0:dv, :] = vt[h * dv:(h + 1) * dv]
        vt_ref[h, dv:dvp, :] = ones_pad


def _proj_in(x2d, wh, wk, wqt, wvt, B, L, H, tm):
    T, D = x2d.shape
    Ch = wh.shape[1]
    Da = wk.shape[1]
    nl = L // tm
    dvp = Da // H + BF16_SUBLANES
    return pl.pallas_call(
        _proj_in_kernel,
        grid=(B, nl),
        in_specs=[
            pl.BlockSpec((tm, D), lambda b, i: (b * nl + i, 0)),
            pl.BlockSpec((D, Ch), lambda b, i: (0, 0)),
            pl.BlockSpec((D, Da), lambda b, i: (0, 0)),
            pl.BlockSpec((Da, D), lambda b, i: (0, 0)),
            pl.BlockSpec((Da, D), lambda b, i: (0, 0)),
        ],
        out_specs=[
            pl.BlockSpec((tm, Ch), lambda b, i: (b * nl + i, 0)),
            pl.BlockSpec((tm, Da), lambda b, i: (b * nl + i, 0)),
            pl.BlockSpec((None, Da, tm), lambda b, i: (b, 0, i)),
            pl.BlockSpec((None, H, dvp, tm), lambda b, i: (b, 0, 0, i)),
        ],
        out_shape=[
            jax.ShapeDtypeStruct((T, Ch), BF16),
            jax.ShapeDtypeStruct((T, Da), BF16),
            jax.ShapeDtypeStruct((B, Da, L), BF16),
            jax.ShapeDtypeStruct((B, H, dvp, L), BF16),
        ],
        compiler_params=_cparams(("arbitrary", "arbitrary")),
        name="proj_in",
    )(x2d, wh, wk, wqt, wvt)


def _hyena_pre_kernel(u_ref, up_ref, un_ref, w_ref, b_ref, vx_ref, x0_ref, *, dh, halo):
    i = pl.program_id(1)
    last = pl.num_programs(1) - 1
    u = u_ref[...].astype(F32)
    tm = u.shape[0]
    prev_row = up_ref[halo - 1:halo, :].astype(F32)
    next_row = un_ref[0:1, :].astype(F32)
    prev_row = jnp.where(i == 0, 0.0, prev_row)
    next_row = jnp.where(i == last, 0.0, next_row)
    rows = lax.broadcasted_iota(jnp.int32, (tm, 1), 0)
    um1 = jnp.where(rows == 0, prev_row, pltpu.roll(u, 1, axis=0))
    up1 = jnp.where(rows == tm - 1, next_row, pltpu.roll(u, tm - 1, axis=0))
    w = w_ref[...]
    y = b_ref[...] + um1 * w[0:1, :]
    y = y + u * w[1:2, :]
    y = y + up1 * w[2:3, :]
    x0_ref[...] = y[:, :dh].astype(BF16)
    vx_ref[...] = y[:, 2 * dh:] * y[:, dh:2 * dh]


def _hyena_pre(uh, conv_w, conv_b, B, L, tm):
    T, Ch = uh.shape
    dh = Ch // 3
    halo = 16
    nl = L // tm
    r = tm // halo
    nhalo = T // halo
    return pl.pallas_call(
        functools.partial(_hyena_pre_kernel, dh=dh, halo=halo),
        grid=(B, nl),
        in_specs=[
            pl.BlockSpec((tm, Ch), lambda b, i: (b * nl + i, 0)),
            pl.BlockSpec((halo, Ch), lambda b, i: (jnp.maximum((b * nl + i) * r - 1, 0), 0)),
            pl.BlockSpec((halo, Ch), lambda b, i: (jnp.minimum((b * nl + i + 1) * r, nhalo - 1), 0)),
            pl.BlockSpec((3, Ch), lambda b, i: (0, 0)),
            pl.BlockSpec((1, Ch), lambda b, i: (0, 0)),
        ],
        out_specs=[
            pl.BlockSpec((tm, dh), lambda b, i: (b * nl + i, 0)),
            pl.BlockSpec((tm, dh), lambda b, i: (b * nl + i, 0)),
        ],
        out_shape=[
            jax.ShapeDtypeStruct((T, dh), F32),
            jax.ShapeDtypeStruct((T, dh), BF16),
        ],
        compiler_params=_cparams(("arbitrary", "arbitrary")),
        name="hyena_pre",
    )(uh, uh, uh, conv_w, conv_b.reshape(1, Ch))


def _dft_tables(L):
    N = 2 * L
    N1 = N // FFT_N2
    NB = L // FFT_N2
    n2 = jnp.arange(FFT_N2, dtype=jnp.int32)[:, None, None]
    k1 = jnp.arange(N1, dtype=jnp.int32)[None, :, None]
    n1 = jnp.arange(N1, dtype=jnp.int32)[None, None, :]
    ph = (k1 * (FFT_N2 * n1 + n2)) % N
    ang = ph.astype(F32) * (2.0 * math.pi / N)
    c, s = jnp.cos(ang), jnp.sin(ang)
    g_full = jnp.concatenate([c, -s], axis=1).astype(BF16)
    g_fwd = g_full[:, :, :NB]
    ci = jnp.transpose(c[:, :, :NB], (0, 2, 1))
    si = jnp.transpose(s[:, :, :NB], (0, 2, 1))
    g_inv = jnp.concatenate([ci, -si], axis=2).astype(BF16)
    a = jnp.arange(FFT_N2, dtype=jnp.int32)
    ang2 = ((a[:, None] * a[None, :]) % FFT_N2).astype(F32) * (2.0 * math.pi / FFT_N2)
    c2, s2 = jnp.cos(ang2), jnp.sin(ang2)
    f2 = jnp.concatenate([jnp.concatenate([c2, s2], 1), jnp.concatenate([-s2, c2], 1)], 0).astype(BF16)
    f2i = jnp.concatenate([jnp.concatenate([c2, -s2], 1), jnp.concatenate([s2, c2], 1)], 0).astype(BF16)
    return g_full, g_fwd, g_inv, f2, f2i


def _filter_spec_kernel(kf_ref, g_ref, f2_ref, h_ref, s_ref, *, n1, inv_n):
    two_n1 = 2 * n1

    def stage_a(n2, carry):
        xs = kf_ref[pl.ds(n2, n1, stride=FFT_N2), :]
        row = pl.multiple_of(n2 * two_n1, two_n1)
        s_ref[pl.ds(row, two_n1), :] = _dot(g_ref[n2], xs.astype(BF16))
        return carry

    lax.fori_loop(0, FFT_N2, stage_a, 0)

    def stage_b(k1, carry):
        zr = s_ref[pl.ds(k1, FFT_N2, stride=two_n1), :]
        zi = s_ref[pl.ds(n1 + k1, FFT_N2, stride=two_n1), :]
        z = jnp.concatenate([zr, zi], axis=0).astype(BF16)
        h_ref[k1] = (_dot(f2_ref[...], z) * inv_n).astype(BF16)
        return carry

    lax.fori_loop(0, n1, stage_b, 0)


def _filter_spec(k_full, g_full, f2, cc):
    N, Dh = k_full.shape
    n1 = N // FFT_N2
    once = pl.Buffered(1)
    return pl.pallas_call(
        functools.partial(_filter_spec_kernel, n1=n1, inv_n=1.0 / N),
        grid=(Dh // cc,),
        in_specs=[
            pl.BlockSpec((N, cc), lambda c: (0, c), pipeline_mode=once),
            pl.BlockSpec((FFT_N2, 2 * n1, n1), lambda c: (0, 0, 0), pipeline_mode=once),
            pl.BlockSpec((2 * FFT_N2, 2 * FFT_N2), lambda c: (0, 0)),
        ],
        out_specs=pl.BlockSpec((n1, 2 * FFT_N2, cc), lambda c: (0, 0, c)),
        out_shape=jax.ShapeDtypeStruct((n1, 2 * FFT_N2, Dh), BF16),
        scratch_shapes=[pltpu.VMEM((FFT_N2 * 2 * n1, cc), F32)],
        compiler_params=_cparams(("arbitrary",)),
        name="filter_spec",
    )(k_full, g_full, f2)


def _fft_conv_kernel(vx_ref, gf_ref, gi_ref, f2_ref, f2i_ref, h_ref, o_ref, s_ref, *, n1, nb):
    two_n1 = 2 * n1

    def stage_a(n2, carry):
        xs = vx_ref[pl.ds(n2, nb, stride=FFT_N2), :]
        row = pl.multiple_of(n2 * two_n1, two_n1)
        s_ref[pl.ds(row, two_n1), :] = _dot(gf_ref[n2], xs.astype(BF16))
        return carry

    lax.fori_loop(0, FFT_N2, stage_a, 0, unroll=4)

    grp = 4

    def load_z(k1):
        zr = s_ref[pl.ds(k1, FFT_N2, stride=two_n1), :]
        zi = s_ref[pl.ds(n1 + k1, FFT_N2, stride=two_n1), :]
        return jnp.concatenate([zr, zi], axis=0).astype(BF16)

    def stage_b(g, zs):
        nxt = jnp.minimum((g + 1) * grp, n1 - grp)
        zn = tuple(load_z(nxt + j) for j in range(grp))
        outs = []
        for j in range(grp):
            sp = _dot(f2_ref[...], zs[j])
            h = h_ref[g * grp + j].astype(F32)
            sr, si = sp[:FFT_N2], sp[FFT_N2:]
            hr, hi = h[:FFT_N2], h[FFT_N2:]
            y = jnp.concatenate([sr * hr - si * hi, sr * hi + si * hr], axis=0).astype(BF16)
            outs.append(_dot(f2i_ref[...], y))
        for j in range(grp):
            k1 = g * grp + j
            s_ref[pl.ds(k1, FFT_N2, stride=two_n1), :] = outs[j][:FFT_N2]
            s_ref[pl.ds(n1 + k1, FFT_N2, stride=two_n1), :] = outs[j][FFT_N2:]
        return zn

    lax.fori_loop(0, n1 // grp, stage_b, tuple(load_z(j) for j in range(grp)))

    def stage_c(n2, carry):
        row = pl.multiple_of(n2 * two_n1, two_n1)
        bb = s_ref[pl.ds(row, two_n1), :].astype(BF16)
        o_ref[pl.ds(n2, nb, stride=FFT_N2), :] = _dot(gi_ref[n2], bb)
        return carry

    lax.fori_loop(0, FFT_N2, stage_c, 0, unroll=4)


def _fft_conv(vx, g_fwd, g_inv, f2, f2i, h_spec, B, L, cc):
    T, Dh = vx.shape
    n1 = 2 * L // FFT_N2
    nb = L // FFT_N2
    once = pl.Buffered(1)
    return pl.pallas_call(
        functools.partial(_fft_conv_kernel, n1=n1, nb=nb),
        grid=(Dh // cc, B),
        in_specs=[
            pl.BlockSpec((L, cc), lambda c, b: (b, c)),
            pl.BlockSpec((FFT_N2, 2 * n1, nb), lambda c, b: (0, 0, 0), pipeline_mode=once),
            pl.BlockSpec((FFT_N2, nb, 2 * n1), lambda c, b: (0, 0, 0), pipeline_mode=once),
            pl.BlockSpec((2 * FFT_N2, 2 * FFT_N2), lambda c, b: (0, 0)),
            pl.BlockSpec((2 * FFT_N2, 2 * FFT_N2), lambda c, b: (0, 0)),
            pl.BlockSpec((n1, 2 * FFT_N2, cc), lambda c, b: (0, 0, c), pipeline_mode=once),
        ],
        out_specs=pl.BlockSpec((L, cc), lambda c, b: (b, c)),
        out_shape=jax.ShapeDtypeStruct((T, Dh), F32),
        scratch_shapes=[pltpu.VMEM((FFT_N2 * 2 * n1, cc), F32)],
        compiler_params=_cparams(("arbitrary", "arbitrary")),
        name="fft_conv",
    )(vx, g_fwd, g_inv, f2, f2i, h_spec)


def _bias_tiles_kernel(table_ref, o_ref, *, t, n_buckets):
    h = pl.program_id(0)
    c = pl.program_id(1)
    nb = n_buckets // 2
    max_exact = nb // 2
    kv = lax.broadcasted_iota(jnp.int32, (t, t), 0)
    q = lax.broadcasted_iota(jnp.int32, (t, t), 1)
    rel = (c - 2) * t + kv - q
    ret = jnp.where(rel > 0, nb, 0)
    n = jnp.abs(rel)
    large = max_exact + (jnp.log(jnp.maximum(n, 1).astype(F32) / max_exact)
                         / math.log(REL_MAX_DIST / max_exact) * (nb - max_exact)).astype(jnp.int32)
    large = jnp.minimum(large, nb - 1)
    bucket = ret + jnp.where(n < max_exact, n, large)
    out = jnp.zeros((t, t), F32)
    for b in range(n_buckets):
        out = jnp.where(bucket == b, table_ref[b, h], out)
    o_ref[...] = out * LOG2E


def _bias_tiles(rel_table, t):
    n_buckets, H = rel_table.shape
    return pl.pallas_call(
        functools.partial(_bias_tiles_kernel, t=t, n_buckets=n_buckets),
        grid=(H, 5),
        in_specs=[pl.BlockSpec(memory_space=pltpu.SMEM)],
        out_specs=pl.BlockSpec((None, None, t, t), lambda h, c: (h, c, 0, 0)),
        out_shape=jax.ShapeDtypeStruct((H, 5, t, t), F32),
        compiler_params=_cparams(("arbitrary", "arbitrary")),
        name="bias_tiles",
    )(rel_table)


def _attn_kernel(lam_ref, qt_ref, k_ref, vt_ref, bias_ref, g_ref, o_ref, acc1_ref, acc2_ref,
                 *, t, nk, hd, out_scale):
    qi = pl.program_id(2)
    dv = 2 * hd
    qt = qt_ref[...]
    half = lax.broadcasted_iota(jnp.int32, qt.shape, 0) < hd
    zero = jnp.zeros_like(qt)
    q1 = jnp.where(half, qt, zero)
    q2 = jnp.where(half, zero, qt)
    acc1_ref[...] = jnp.zeros_like(acc1_ref)
    acc2_ref[...] = jnp.zeros_like(acc2_ref)

    def online(s, shift, m, acc_ref, vb):
        m_tile = jnp.max(s, axis=0, keepdims=True)
        if shift is not None:
            m_tile = m_tile + shift
        m_new = jnp.maximum(m, m_tile)
        alpha = jnp.exp2(m - m_new)
        p = jnp.exp2(s - (m_new if shift is None else m_new - shift))
        acc_ref[...] = alpha * acc_ref[...] + _dot(vb, p.astype(BF16))
        return m_new

    def make_step(near, far_class):
        def step(ki, carry):
            m1, m2 = carry
            off = pl.multiple_of(ki * t, t)
            kb = k_ref[pl.ds(off, t), :]
            vb = vt_ref[:, pl.ds(off, t)]
            s1 = _dot(kb, q1)
            s2 = _dot(kb, q2)
            if near:
                bias = bias_ref[jnp.clip(ki - qi, -2, 2) + 2]
                s1 = s1 + bias
                s2 = s2 + bias
                shift = None
            else:
                shift = bias_ref[far_class, 0:1, 0:1]
            m1 = online(s1, shift, m1, acc1_ref, vb)
            m2 = online(s2, shift, m2, acc2_ref, vb)
            return m1, m2
        return step

    neg = jnp.full((1, t), -jnp.inf, F32)
    lo = jnp.maximum(qi - 1, 0)
    hi = jnp.minimum(qi + 2, nk)
    carry = lax.fori_loop(0, lo, make_step(False, 0), (neg, neg))
    carry = lax.fori_loop(lo, hi, make_step(True, None), carry)
    m1, m2 = lax.fori_loop(hi, nk, make_step(False, 4), carry)
    a1 = acc1_ref[...]
    a2 = acc2_ref[...]
    o = a1[:dv] / a1[dv:dv + 1] - lam_ref[0] * (a2[:dv] / a2[dv:dv + 1])
    ms = jnp.mean(o * o, axis=0, keepdims=True)
    o = o * lax.rsqrt(ms + LN_EPS) * g_ref[...] * out_scale
    o_ref[...] = o.T.astype(BF16)


def _attention(lam, qt, k, vt, bias, g_col, B, L, H, hd, t, lam_init):
    Da = H * 2 * hd
    nq = L // t
    dvp = vt.shape[2]
    k3 = k.reshape(B, L, Da)
    out = pl.pallas_call(
        functools.partial(_attn_kernel, t=t, nk=nq, hd=hd, out_scale=1.0 - lam_init),
        grid=(B, H, nq),
        in_specs=[
            pl.BlockSpec(memory_space=pltpu.SMEM),
            pl.BlockSpec((None, 2 * hd, t), lambda b, h, q: (b, h, q)),
            pl.BlockSpec((None, L, 2 * hd), lambda b, h, q: (b, 0, h)),
            pl.BlockSpec((None, None, dvp, L), lambda b, h, q: (b, h, 0, 0)),
            pl.BlockSpec((None, 5, t, t), lambda b, h, q: (h, 0, 0, 0)),
            pl.BlockSpec((2 * hd, 1), lambda b, h, q: (0, 0)),
        ],
        out_specs=pl.BlockSpec((None, t, 2 * hd), lambda b, h, q: (b, q, h)),
        out_shape=jax.ShapeDtypeStruct((B, L, Da), BF16),
        scratch_shapes=[pltpu.VMEM((dvp, t), F32), pltpu.VMEM((dvp, t), F32)],
        compiler_params=_cparams(("arbitrary", "arbitrary", "arbitrary")),
        name="diff_attn",
    )(lam, qt, k3, vt, bias, g_col)
    return out.reshape(B * L, Da)


def _proj_out_kernel(cy_ref, vx_ref, x0_ref, hb_ref, ya_ref, x_ref, woh_ref, woa_ref, g_ref, b_ref,
                     wrc_ref, wrh_ref, x1_ref, eid_ref, gate_ref, *, alpha, n_groups, per_group):
    yh = ((cy_ref[...] + vx_ref[...] * hb_ref[...]) * x0_ref[...].astype(F32)).astype(BF16)
    mix = _dot(yh, woh_ref[...]) + _dot(ya_ref[...], woa_ref[...])
    x1 = _layer_norm_rows(alpha * x_ref[...] + mix, g_ref[...], b_ref[...])
    x1_ref[...] = x1
    hi = x1.astype(BF16)
    lo = (x1 - hi.astype(F32)).astype(BF16)
    lg2 = _dot(hi, wrc_ref[...])
    lg = lg2[:, :LANES] + lg2[:, LANES:] + _dot(lo, wrh_ref[...])
    col = lax.broadcasted_iota(jnp.int32, lg.shape, 1).astype(F32)
    far = jnp.float32(LANES)
    neg_inf = jnp.float32(-jnp.inf)

    def first_col(mask):
        return jnp.min(jnp.where(mask, col, far), axis=-1, keepdims=True)

    gmask = col < n_groups
    glog = jnp.where(gmask, lg, neg_inf)
    gmax = jnp.max(glog, axis=-1, keepdims=True)
    g_sel = first_col(glog == gmax)
    g_w = 1.0 / jnp.sum(jnp.where(gmask, jnp.exp(glog - gmax), 0.0), axis=-1, keepdims=True)
    lo_col = n_groups + g_sel * per_group
    emask = (col >= lo_col) & (col < lo_col + per_group)
    elog = jnp.where(emask, lg, neg_inf)
    emax = jnp.max(elog, axis=-1, keepdims=True)
    eexp = jnp.where(emask, jnp.exp(elog - emax), 0.0)
    prob = eexp / jnp.sum(eexp, axis=-1, keepdims=True)
    p1 = jnp.max(prob, axis=-1, keepdims=True)
    i1 = first_col(emask & (prob == p1))
    mask2 = emask & (col != i1)
    p2 = jnp.max(jnp.where(mask2, prob, -1.0), axis=-1, keepdims=True)
    i2 = first_col(mask2 & (prob == p2))
    psum = p1 + p2
    eid = jnp.where(col == 0, i1 - n_groups, jnp.where(col == 1, i2 - n_groups, 0.0))
    eid_ref[...] = eid.astype(jnp.int32)
    gate_ref[...] = jnp.where(col == 0, g_w * p1 / psum, jnp.where(col == 1, g_w * p2 / psum, 0.0))


def _proj_out(cy, vx, x0c, hb, ya, x, woh, woa, g, b, wrc, wrh, alpha, n_groups, per_group, tm):
    T, D = x.shape
    Dh = cy.shape[1]
    Da = ya.shape[1]
    row = lambda i: (i, 0)
    fix = lambda i: (0, 0)
    return pl.pallas_call(
        functools.partial(_proj_out_kernel, alpha=alpha, n_groups=n_groups, per_group=per_group),
        grid=(T // tm,),
        in_specs=[
            pl.BlockSpec((tm, Dh), row),
            pl.BlockSpec((tm, Dh), row),
            pl.BlockSpec((tm, Dh), row),
            pl.BlockSpec((1, Dh), fix),
            pl.BlockSpec((tm, Da), row),
            pl.BlockSpec((tm, D), row),
            pl.BlockSpec((Dh, D), fix),
            pl.BlockSpec((Da, D), fix),
            pl.BlockSpec((1, D), fix),
            pl.BlockSpec((1, D), fix),
            pl.BlockSpec((D, 2 * LANES), fix),
            pl.BlockSpec((D, LANES), fix),
        ],
        out_specs=[
            pl.BlockSpec((tm, D), row),
            pl.BlockSpec((tm, LANES), row),
            pl.BlockSpec((tm, LANES), row),
        ],
        out_shape=[
            jax.ShapeDtypeStruct((T, D), F32),
            jax.ShapeDtypeStruct((T, LANES), jnp.int32),
            jax.ShapeDtypeStruct((T, LANES), F32),
        ],
        compiler_params=_cparams(("arbitrary",)),
        name="proj_out_ln_router",
    )(cy, vx, x0c, hb, ya, x, woh, woa, g, b, wrc, wrh)


def _moe_kernel(be_ref, nu_ref, src_ref, srcn_ref, dst_ref, x_hbm, wg_ref, wu_ref, wd_ref, out_hbm,
                xbuf, ybuf, wgb, wub, wdb, sem_in, sem_out, *, bm, unroll):
    i = pl.program_id(0)
    n_used = nu_ref[0]
    slot = i % 2

    def start_gather(idx_ref, s):
        def body(r, c):
            pltpu.make_async_copy(x_hbm.at[pl.ds(idx_ref[0, r], 1), :],
                                  xbuf.at[s, pl.ds(r, 1), :], sem_in.at[s]).start()
            return c

        lax.fori_loop(0, bm, body, 0, unroll=unroll)

    def wait_gather(s):
        pltpu.make_async_copy(x_hbm.at[pl.ds(0, bm), :], xbuf.at[s], sem_in.at[s]).wait()

    def wait_scatter():
        pltpu.make_async_copy(ybuf, out_hbm.at[pl.ds(0, bm), :], sem_out).wait()

    @pl.when(i == 0)
    def _():
        start_gather(src_ref, 0)

    @pl.when(i + 1 < n_used)
    def _():
        start_gather(srcn_ref, 1 - slot)

    @pl.when(i < n_used)
    def _():
        changed = jnp.logical_or(i == 0, be_ref[i] != be_ref[jnp.maximum(i - 1, 0)])

        @pl.when(changed)
        def _():
            wgb[...] = wg_ref[...].astype(BF16)
            wub[...] = wu_ref[...].astype(BF16)
            wdb[...] = wd_ref[...].astype(BF16)

        wait_gather(slot)
        xb = xbuf[slot].astype(BF16)
        hg = _dot(xb, wgb[...])
        hu = _dot(xb, wub[...])
        act = (hg / (1.0 + jnp.exp(-hg))) * hu
        y = _dot(act.astype(BF16), wdb[...])

        @pl.when(i > 0)
        def _():
            wait_scatter()

        ybuf[...] = y

        def scatter_row(r, c):
            pltpu.make_async_copy(ybuf.at[pl.ds(r, 1), :],
                                  out_hbm.at[pl.ds(dst_ref[0, r], 1), :], sem_out).start()
            return c

        lax.fori_loop(0, bm, scatter_row, 0, unroll=unroll)

        @pl.when(i == n_used - 1)
        def _():
            wait_scatter()


def _moe_experts(block_e, n_used, src, dst, x1, w_gate, w_up, w_down, n_rows_out, bm):
    n_blocks = block_e.shape[0]
    T, D = x1.shape
    E, _, De = w_gate.shape
    idx_block = (None, 1, bm)
    grid_spec = pltpu.PrefetchScalarGridSpec(
        num_scalar_prefetch=2,
        grid=(n_blocks,),
        in_specs=[
            pl.BlockSpec(idx_block, lambda i, be, nu: (i, 0, 0), memory_space=pltpu.SMEM),
            pl.BlockSpec(idx_block, lambda i, be, nu: (jnp.minimum(i + 1, n_blocks - 1), 0, 0),
                         memory_space=pltpu.SMEM),
            pl.BlockSpec(idx_block, lambda i, be, nu: (i, 0, 0), memory_space=pltpu.SMEM),
            pl.BlockSpec(memory_space=pl.ANY),
            pl.BlockSpec((None, D, De), lambda i, be, nu: (be[i], 0, 0)),
            pl.BlockSpec((None, D, De), lambda i, be, nu: (be[i], 0, 0)),
            pl.BlockSpec((None, De, D), lambda i, be, nu: (be[i], 0, 0)),
        ],
        out_specs=pl.BlockSpec(memory_space=pl.ANY),
        scratch_shapes=[
            pltpu.VMEM((2, bm, D), F32),
            pltpu.VMEM((bm, D), F32),
            pltpu.VMEM((D, De), BF16),
            pltpu.VMEM((D, De), BF16),
            pltpu.VMEM((De, D), BF16),
            pltpu.SemaphoreType.DMA((2,)),
            pltpu.SemaphoreType.DMA(()),
        ],
    )
    return pl.pallas_call(
        functools.partial(_moe_kernel, bm=bm, unroll=8),
        grid_spec=grid_spec,
        out_shape=jax.ShapeDtypeStruct((n_rows_out, D), F32),
        compiler_params=_cparams(("arbitrary",)),
        name="moe_experts",
    )(block_e, n_used, src, src, dst, x1, w_gate, w_up, w_down)


def _combine_kernel(ya_ref, yb_ref, gate_ref, x1_ref, g_ref, b_ref, o_ref, *, alpha):
    gates = gate_ref[...]
    y = gates[:, 0:1] * ya_ref[...] + gates[:, 1:2] * yb_ref[...]
    o_ref[...] = _layer_norm_rows(alpha * x1_ref[...] + y, g_ref[...], b_ref[...])


def _combine_ln(y2, gates, x1, g, b, alpha, tm):
    T, D = x1.shape
    nt = T // tm
    row = lambda i: (i, 0)
    fix = lambda i: (0, 0)
    return pl.pallas_call(
        functools.partial(_combine_kernel, alpha=alpha),
        grid=(nt,),
        in_specs=[
            pl.BlockSpec((tm, D), row),
            pl.BlockSpec((tm, D), lambda i: (nt + i, 0)),
            pl.BlockSpec((tm, LANES), row),
            pl.BlockSpec((tm, D), row),
            pl.BlockSpec((1, D), fix),
            pl.BlockSpec((1, D), fix),
        ],
        out_specs=pl.BlockSpec((tm, D), row),
        out_shape=jax.ShapeDtypeStruct((T, D), F32),
        compiler_params=_cparams(("arbitrary",)),
        name="moe_combine_ln",
    )(y2, y2, gates, x1, g, b)


def _hyena_filter(L, dh, f_w1, f_b1, f_freq, f_w2, f_b2, f_w3):
    emb = f_w1.shape[0]
    bands = (emb - 1) // 2
    t = jnp.linspace(0.0, 1.0, L, dtype=F32)[:, None]
    w = (2.0 * math.pi / L) * jnp.arange(L, dtype=F32)[:, None]
    f = jnp.linspace(1e-4, bands - 1, bands, dtype=F32)[None, :]
    z = jnp.concatenate([t, jnp.cos(f * w), -jnp.sin(f * w)], axis=-1)
    hp = lax.Precision.HIGHEST
    h = jnp.sin(f_freq * (jnp.dot(z, f_w1, precision=hp) + f_b1))
    h = jnp.sin(f_freq * (jnp.dot(h, f_w2, precision=hp) + f_b2))
    h = jnp.dot(h, f_w3, precision=hp)
    max_decay = math.log(DECAY_TARGET) / FAST_DECAY_PCT
    min_decay = math.log(DECAY_TARGET) / SLOW_DECAY_PCT
    deltas = jnp.abs(jnp.linspace(min_decay, max_decay, dh, dtype=F32))
    decay = jnp.exp(-t * deltas)
    h_fwd = h[:, :dh] * decay
    h_bwd = h[:, dh:] * decay
    k_full = jnp.concatenate([h_fwd, jnp.zeros((1, dh), F32), h_bwd[1:][::-1]], axis=0)
    return k_full / jnp.sum(jnp.abs(k_full), axis=0, keepdims=True)


def _dispatch_tables(eid, n_experts, bm):
    T = eid.shape[0]
    A = T * TOP_K
    flat_e = eid.reshape(A)
    order = jnp.argsort(flat_e, stable=True).astype(jnp.int32)
    se = flat_e[order]
    counts = jnp.zeros((n_experts,), jnp.int32).at[flat_e].add(1)
    starts = jnp.cumsum(counts) - counts
    pcounts = (counts + bm - 1) // bm * bm
    pends = jnp.cumsum(pcounts)
    pstarts = pends - pcounts
    dest = pstarts[se] + (jnp.arange(A, dtype=jnp.int32) - starts[se])
    n_blocks = -(-A // bm) + n_experts
    P = n_blocks * bm
    row_asg = jnp.full((P,), -1, jnp.int32).at[dest].set(order)
    pad = row_asg < 0
    src = jnp.maximum(row_asg, 0) // TOP_K
    dst = jnp.where(pad, A + jnp.cumsum(pad.astype(jnp.int32)) - 1, (row_asg % TOP_K) * T + row_asg // TOP_K)
    block_e = jnp.minimum(
        jnp.searchsorted(pends, jnp.arange(n_blocks, dtype=jnp.int32) * bm, side="right"),
        n_experts - 1).astype(jnp.int32)
    n_used = (pends[-1] // bm).astype(jnp.int32).reshape(1)
    return block_e, n_used, src.reshape(n_blocks, 1, bm), dst.astype(jnp.int32).reshape(n_blocks, 1, bm), P


def _pick(n, pref):
    t = min(n, pref)
    assert n % t == 0, (n, t)
    return t


def _trunk(xa, xb, p):
    Ba, L, D = xa.shape
    Bb = xb.shape[0]
    assert xb.shape[1:] == (L, D)
    B = Ba + Bb
    T = B * L
    depth = p["w_in"].shape[0]
    dh = p["h_bias"].shape[1]
    H = p["rel_table"].shape[1]
    hd = p["lam_q1"].shape[1]
    da = H * 2 * hd
    n_groups = p["w_route_group"].shape[2]
    n_experts = p["w_route_expert"].shape[2]
    per_group = n_experts // n_groups
    alpha = (2.0 * depth) ** 0.25
    assert L % FFT_N2 == 0 and (L // FFT_N2) % 8 == 0
    assert 2 * hd == LANES and dh % LANES == 0 and n_groups + n_experts <= LANES

    tm = _pick(L, 512)
    t_attn = _pick(L, 512)
    bm = 512

    xs = _ln_embed(xa.reshape(Ba * L, D), xb.reshape(Bb * L, D), p["ln_emb_g"], p["ln_emb_b"], tm)
    tables = _dft_tables(L)
    g_full, g_fwd, g_inv, f2, f2i = tables
    bias = _bias_tiles(p["rel_table"], t_attn)

    for l in range(depth):
        lam_init = 0.8 - 0.6 * math.exp(-0.3 * l)
        w_in = p["w_in"][l]
        wh = w_in[:, :3 * dh].astype(BF16)
        wa = w_in[:, 3 * dh:]
        wqt = (wa[:, :da] * (hd ** -0.5 * LOG2E)).T.astype(BF16)
        wk = wa[:, da:2 * da].astype(BF16)
        wvt = wa[:, 2 * da:].T.astype(BF16)
        uh, k, qt, vt = _proj_in(xs, wh, wk, wqt, wvt, B, L, H, tm)

        vx, x0c = _hyena_pre(uh, p["conv_w"][l], p["conv_b"][l], B, L, tm)
        k_full = _hyena_filter(L, dh, p["f_w1"][l], p["f_b1"][l], p["f_freq"][l], p["f_w2"][l],
                               p["f_b2"][l], p["f_w3"][l])
        h_spec = _filter_spec(k_full, g_full, f2, LANES)
        cy = _fft_conv(vx, g_fwd, g_inv, f2, f2i, h_spec, B, L, LANES)

        lam = (jnp.exp(jnp.sum(p["lam_q1"][l] * p["lam_k1"][l]))
               - jnp.exp(jnp.sum(p["lam_q2"][l] * p["lam_k2"][l])) + lam_init).reshape(1).astype(F32)
        ya = _attention(lam, qt, k, vt, bias, p["subln_g"][l].reshape(2 * hd, 1), B, L, H, hd, t_attn,
                        lam_init)

        w_out = p["w_out"][l]
        w_r = jnp.concatenate([p["w_route_group"][l], p["w_route_expert"][l]], axis=1)
        w_r = jnp.pad(w_r, ((0, 0), (0, LANES - w_r.shape[1])))
        w_r_hi = w_r.astype(BF16)
        w_r_lo = (w_r - w_r_hi.astype(F32)).astype(BF16)
        x1, eid, gates = _proj_out(
            cy, vx, x0c, p["h_bias"][l].reshape(1, dh), ya, xs,
            w_out[:dh].astype(BF16), w_out[dh:].astype(BF16),
            p["ln1_g"][l].reshape(1, D), p["ln1_b"][l].reshape(1, D),
            jnp.concatenate([w_r_hi, w_r_lo], axis=1), w_r_hi,
            alpha, n_groups, per_group, tm)

        block_e, n_used, src, dst, n_rows = _dispatch_tables(eid[:, :TOP_K], n_experts, bm)
        y2 = _moe_experts(block_e, n_used, src, dst, x1, p["w_gate"][l], p["w_up"][l], p["w_down"][l],
                          n_rows, bm)
        xs = _combine_ln(y2, gates, x1, p["ln2_g"][l].reshape(1, D), p["ln2_b"][l].reshape(1, D),
                         alpha, tm)
    return xs[:Ba * L].reshape(Ba, L, D), xs[Ba * L:].reshape(Bb, L, D)


def kernel(x_prompt, x_sample, ln_emb_g, ln_emb_b, rel_table, w_in, conv_w, conv_b, f_w1, f_b1, f_freq,
           f_w2, f_b2, f_w3, h_bias, lam_q1, lam_k1, lam_q2, lam_k2, subln_g, w_out, ln1_g, ln1_b,
           w_route_group, w_route_expert, w_gate, w_up, w_down, ln2_g, ln2_b):
    p = dict(ln_emb_g=ln_emb_g, ln_emb_b=ln_emb_b, rel_table=rel_table, w_in=w_in, conv_w=conv_w,
             conv_b=conv_b, f_w1=f_w1, f_b1=f_b1, f_freq=f_freq, f_w2=f_w2, f_b2=f_b2, f_w3=f_w3,
             h_bias=h_bias, lam_q1=lam_q1, lam_k1=lam_k1, lam_q2=lam_q2, lam_k2=lam_k2,
             subln_g=subln_g, w_out=w_out, ln1_g=ln1_g, ln1_b=ln1_b, w_route_group=w_route_group,
             w_route_expert=w_route_expert, w_gate=w_gate, w_up=w_up, w_down=w_down, ln2_g=ln2_g,
             ln2_b=ln2_b)
    return _trunk(x_prompt, x_sample, p)
```

```python
import functools
import math

import jax
import jax.numpy as jnp
from jax import lax
from jax.experimental import pallas as pl
from jax.experimental.pallas import tpu as pltpu

F32 = jnp.float32
BF16 = jnp.bfloat16

LN_EPS = 1e-5
REL_MAX_DIST = 128
TOP_K = 2
DECAY_TARGET = 1e-2
FAST_DECAY_PCT = 0.3
SLOW_DECAY_PCT = 1.5

LANES = 128
BF16_SUBLANES = 16
LOG2E = 1.4426950408889634
FFT_N2 = 128
VMEM_LIMIT_BYTES = 56 * 1024 * 1024


def _cparams(sem):
    return pltpu.CompilerParams(dimension_semantics=sem, vmem_limit_bytes=VMEM_LIMIT_BYTES)


def _dot(a, b):
    return jnp.dot(a, b, preferred_element_type=F32)


def _layer_norm_rows(z, g, b):
    mu = jnp.mean(z, axis=-1, keepdims=True)
    zc = z - mu
    var = jnp.mean(zc * zc, axis=-1, keepdims=True)
    return zc * lax.rsqrt(var + LN_EPS) * g + b


def _ln_kernel(xa_ref, xb_ref, g_ref, b_ref, o_ref, *, na):
    i = pl.program_id(0)

    @pl.when(i < na)
    def _():
        o_ref[...] = _layer_norm_rows(xa_ref[...], g_ref[...], b_ref[...])

    @pl.when(i >= na)
    def _():
        o_ref[...] = _layer_norm_rows(xb_ref[...], g_ref[...], b_ref[...])


def _ln_embed(xa, xb, g, b, tm):
    Ta, D = xa.shape
    Tb = xb.shape[0]
    na = Ta // tm
    nb = Tb // tm
    return pl.pallas_call(
        functools.partial(_ln_kernel, na=na),
        grid=(na + nb,),
        in_specs=[
            pl.BlockSpec((tm, D), lambda i: (jnp.minimum(i, na - 1), 0)),
            pl.BlockSpec((tm, D), lambda i: (jnp.maximum(i - na, 0), 0)),
            pl.BlockSpec((1, D), lambda i: (0, 0)),
            pl.BlockSpec((1, D), lambda i: (0, 0)),
        ],
        out_specs=pl.BlockSpec((tm, D), lambda i: (i, 0)),
        out_shape=jax.ShapeDtypeStruct((Ta + Tb, D), F32),
        compiler_params=_cparams(("arbitrary",)),
        name="ln_embed",
    )(xa, xb, g.reshape(1, D), b.reshape(1, D))


def _proj_in_kernel(x_ref, wh_ref, wk_ref, wqt_ref, wvt_ref, uh_ref, k_ref, qt_ref, vt_ref):
    xb = x_ref[...].astype(BF16)
    uh_ref[...] = _dot(xb, wh_ref[...]).astype(BF16)
    k_ref[...] = _dot(xb, wk_ref[...]).astype(BF16)
    nt = (((1,), (1,)), ((), ()))
    qt_ref[...] = lax.dot_general(wqt_ref[...], xb, nt, preferred_element_type=F32).astype(BF16)
    vt = lax.dot_general(wvt_ref[...], xb, nt, preferred_element_type=F32).astype(BF16)
    n_heads, dvp, tm = vt_ref.shape
    dv = vt.shape[0] // n_heads
    pad_rows = lax.broadcasted_iota(jnp.int32, (dvp - dv, tm), 0)
    ones_pad = jnp.where(pad_rows == 0, 1.0, 0.0).astype(BF16)
    for h in range(n_heads):
        vt_ref[h, 0:dv, :] = vt[h * dv:(h + 1) * dv]
        vt_ref[h, dv:dvp, :] = ones_pad


def _proj_in(x2d, wh, wk, wqt, wvt, B, L, H, tm):
    T, D = x2d.shape
    Ch = wh.shape[1]
    Da = wk.shape[1]
    nl = L // tm
    dvp = Da // H + BF16_SUBLANES
    return pl.pallas_call(
        _proj_in_kernel,
        grid=(B, nl),
        in_specs=[
            pl.BlockSpec((tm, D), lambda b, i: (b * nl + i, 0)),
            pl.BlockSpec((D, Ch), lambda b, i: (0, 0)),
            pl.BlockSpec((D, Da), lambda b, i: (0, 0)),
            pl.BlockSpec((Da, D), lambda b, i: (0, 0)),
            pl.BlockSpec((Da, D), lambda b, i: (0, 0)),
        ],
        out_specs=[
            pl.BlockSpec((tm, Ch), lambda b, i: (b * nl + i, 0)),
            pl.BlockSpec((tm, Da), lambda b, i: (b * nl + i, 0)),
            pl.BlockSpec((None, Da, tm), lambda b, i: (b, 0, i)),
            pl.BlockSpec((None, H, dvp, tm), lambda b, i: (b, 0, 0, i)),
        ],
        out_shape=[
            jax.ShapeDtypeStruct((T, Ch), BF16),
            jax.ShapeDtypeStruct((T, Da), BF16),
            jax.ShapeDtypeStruct((B, Da, L), BF16),
            jax.ShapeDtypeStruct((B, H, dvp, L), BF16),
        ],
        compiler_params=_cparams(("arbitrary", "arbitrary")),
        name="proj_in",
    )(x2d, wh, wk, wqt, wvt)


def _hyena_pre_kernel(u_ref, up_ref, un_ref, w_ref, b_ref, vx_ref, x0_ref, *, dh, halo):
    i = pl.program_id(1)
    last = pl.num_programs(1) - 1
    u = u_ref[...].astype(F32)
    tm = u.shape[0]
    prev_row = up_ref[halo - 1:halo, :].astype(F32)
    next_row = un_ref[0:1, :].astype(F32)
    prev_row = jnp.where(i == 0, 0.0, prev_row)
    next_row = jnp.where(i == last, 0.0, next_row)
    rows = lax.broadcasted_iota(jnp.int32, (tm, 1), 0)
    um1 = jnp.where(rows == 0, prev_row, pltpu.roll(u, 1, axis=0))
    up1 = jnp.where(rows == tm - 1, next_row, pltpu.roll(u, tm - 1, axis=0))
    w = w_ref[...]
    y = b_ref[...] + um1 * w[0:1, :]
    y = y + u * w[1:2, :]
    y = y + up1 * w[2:3, :]
    x0_ref[...] = y[:, :dh].astype(BF16)
    vx_ref[...] = y[:, 2 * dh:] * y[:, dh:2 * dh]


def _hyena_pre(uh, conv_w, conv_b, B, L, tm):
    T, Ch = uh.shape
    dh = Ch // 3
    halo = 16
    nl = L // tm
    r = tm // halo
    nhalo = T // halo
    return pl.pallas_call(
        functools.partial(_hyena_pre_kernel, dh=dh, halo=halo),
        grid=(B, nl),
        in_specs=[
            pl.BlockSpec((tm, Ch), lambda b, i: (b * nl + i, 0)),
            pl.BlockSpec((halo, Ch), lambda b, i: (jnp.maximum((b * nl + i) * r - 1, 0), 0)),
            pl.BlockSpec((halo, Ch), lambda b, i: (jnp.minimum((b * nl + i + 1) * r, nhalo - 1), 0)),
            pl.BlockSpec((3, Ch), lambda b, i: (0, 0)),
            pl.BlockSpec((1, Ch), lambda b, i: (0, 0)),
        ],
        out_specs=[
            pl.BlockSpec((tm, dh), lambda b, i: (b * nl + i, 0)),
            pl.BlockSpec((tm, dh), lambda b, i: (b * nl + i, 0)),
        ],
        out_shape=[
            jax.ShapeDtypeStruct((T, dh), F32),
            jax.ShapeDtypeStruct((T, dh), BF16),
        ],
        compiler_params=_cparams(("arbitrary", "arbitrary")),
        name="hyena_pre",
    )(uh, uh, uh, conv_w, conv_b.reshape(1, Ch))


def _dft_tables(L):
    N = 2 * L
    N1 = N // FFT_N2
    NB = L // FFT_N2
    n2 = jnp.arange(FFT_N2, dtype=jnp.int32)[:, None, None]
    k1 = jnp.arange(N1, dtype=jnp.int32)[None, :, None]
    n1 = jnp.arange(N1, dtype=jnp.int32)[None, None, :]
    ph = (k1 * (FFT_N2 * n1 + n2)) % N
    ang = ph.astype(F32) * (2.0 * math.pi / N)
    c, s = jnp.cos(ang), jnp.sin(ang)
    g_full = jnp.concatenate([c, -s], axis=1).astype(BF16)
    g_fwd = g_full[:, :, :NB]
    ci = jnp.transpose(c[:, :, :NB], (0, 2, 1))
    si = jnp.transpose(s[:, :, :NB], (0, 2, 1))
    g_inv = jnp.concatenate([ci, -si], axis=2).astype(BF16)
    a = jnp.arange(FFT_N2, dtype=jnp.int32)
    ang2 = ((a[:, None] * a[None, :]) % FFT_N2).astype(F32) * (2.0 * math.pi / FFT_N2)
    c2, s2 = jnp.cos(ang2), jnp.sin(ang2)
    f2 = jnp.concatenate([jnp.concatenate([c2, s2], 1), jnp.concatenate([-s2, c2], 1)], 0).astype(BF16)
    f2i = jnp.concatenate([jnp.concatenate([c2, -s2], 1), jnp.concatenate([s2, c2], 1)], 0).astype(BF16)
    return g_full, g_fwd, g_inv, f2, f2i


def _filter_spec_kernel(kf_ref, g_ref, f2_ref, h_ref, s_ref, *, n1, inv_n):
    two_n1 = 2 * n1

    def stage_a(n2, carry):
        xs = kf_ref[pl.ds(n2, n1, stride=FFT_N2), :]
        row = pl.multiple_of(n2 * two_n1, two_n1)
        s_ref[pl.ds(row, two_n1), :] = _dot(g_ref[n2], xs.astype(BF16))
        return carry

    lax.fori_loop(0, FFT_N2, stage_a, 0, unroll=4)

    def stage_b(k1, carry):
        zr = s_ref[pl.ds(k1, FFT_N2, stride=two_n1), :]
        zi = s_ref[pl.ds(n1 + k1, FFT_N2, stride=two_n1), :]
        z = jnp.concatenate([zr, zi], axis=0).astype(BF16)
        h_ref[k1] = (_dot(f2_ref[...], z) * inv_n).astype(BF16)
        return carry

    lax.fori_loop(0, n1, stage_b, 0, unroll=4)


def _filter_spec(k_full, g_full, f2, cc):
    N, Dh = k_full.shape
    n1 = N // FFT_N2
    once = pl.Buffered(1)
    return pl.pallas_call(
        functools.partial(_filter_spec_kernel, n1=n1, inv_n=1.0 / N),
        grid=(Dh // cc,),
        in_specs=[
            pl.BlockSpec((N, cc), lambda c: (0, c), pipeline_mode=once),
            pl.BlockSpec((FFT_N2, 2 * n1, n1), lambda c: (0, 0, 0), pipeline_mode=once),
            pl.BlockSpec((2 * FFT_N2, 2 * FFT_N2), lambda c: (0, 0)),
        ],
        out_specs=pl.BlockSpec((n1, 2 * FFT_N2, cc), lambda c: (0, 0, c)),
        out_shape=jax.ShapeDtypeStruct((n1, 2 * FFT_N2, Dh), BF16),
        scratch_shapes=[pltpu.VMEM((FFT_N2 * 2 * n1, cc), F32)],
        compiler_params=_cparams(("arbitrary",)),
        name="filter_spec",
    )(k_full, g_full, f2)


def _fft_conv_kernel(vx_ref, gf_ref, gi_ref, f2_ref, f2i_ref, h_ref, o_ref, s_ref, *, n1, nb):
    two_n1 = 2 * n1

    def stage_a(n2, carry):
        xs = vx_ref[pl.ds(n2, nb, stride=FFT_N2), :]
        row = pl.multiple_of(n2 * two_n1, two_n1)
        s_ref[pl.ds(row, two_n1), :] = _dot(gf_ref[n2], xs.astype(BF16))
        return carry

    lax.fori_loop(0, FFT_N2, stage_a, 0, unroll=4)

    grp = 4

    def load_z(k1):
        zr = s_ref[pl.ds(k1, FFT_N2, stride=two_n1), :]
        zi = s_ref[pl.ds(n1 + k1, FFT_N2, stride=two_n1), :]
        return jnp.concatenate([zr, zi], axis=0).astype(BF16)

    def stage_b(g, zs):
        nxt = jnp.minimum((g + 1) * grp, n1 - grp)
        zn = tuple(load_z(nxt + j) for j in range(grp))
        outs = []
        for j in range(grp):
            sp = _dot(f2_ref[...], zs[j])
            h = h_ref[g * grp + j].astype(F32)
            sr, si = sp[:FFT_N2], sp[FFT_N2:]
            hr, hi = h[:FFT_N2], h[FFT_N2:]
            y = jnp.concatenate([sr * hr - si * hi, sr * hi + si * hr], axis=0).astype(BF16)
            outs.append(_dot(f2i_ref[...], y))
        for j in range(grp):
            k1 = g * grp + j
            s_ref[pl.ds(k1, FFT_N2, stride=two_n1), :] = outs[j][:FFT_N2]
            s_ref[pl.ds(n1 + k1, FFT_N2, stride=two_n1), :] = outs[j][FFT_N2:]
        return zn

    lax.fori_loop(0, n1 // grp, stage_b, tuple(load_z(j) for j in range(grp)))

    def stage_c(n2, carry):
        row = pl.multiple_of(n2 * two_n1, two_n1)
        bb = s_ref[pl.ds(row, two_n1), :].astype(BF16)
        o_ref[pl.ds(n2, nb, stride=FFT_N2), :] = _dot(gi_ref[n2], bb)
        return carry

    lax.fori_loop(0, FFT_N2, stage_c, 0, unroll=4)


def _fft_conv(vx, g_fwd, g_inv, f2, f2i, h_spec, B, L, cc):
    T, Dh = vx.shape
    n1 = 2 * L // FFT_N2
    nb = L // FFT_N2
    once = pl.Buffered(1)
    return pl.pallas_call(
        functools.partial(_fft_conv_kernel, n1=n1, nb=nb),
        grid=(Dh // cc, B),
        in_specs=[
            pl.BlockSpec((L, cc), lambda c, b: (b, c)),
            pl.BlockSpec((FFT_N2, 2 * n1, nb), lambda c, b: (0, 0, 0), pipeline_mode=once),
            pl.BlockSpec((FFT_N2, nb, 2 * n1), lambda c, b: (0, 0, 0), pipeline_mode=once),
            pl.BlockSpec((2 * FFT_N2, 2 * FFT_N2), lambda c, b: (0, 0)),
            pl.BlockSpec((2 * FFT_N2, 2 * FFT_N2), lambda c, b: (0, 0)),
            pl.BlockSpec((n1, 2 * FFT_N2, cc), lambda c, b: (0, 0, c), pipeline_mode=once),
        ],
        out_specs=pl.BlockSpec((L, cc), lambda c, b: (b, c)),
        out_shape=jax.ShapeDtypeStruct((T, Dh), F32),
        scratch_shapes=[pltpu.VMEM((FFT_N2 * 2 * n1, cc), F32)],
        compiler_params=_cparams(("arbitrary", "arbitrary")),
        name="fft_conv",
    )(vx, g_fwd, g_inv, f2, f2i, h_spec)


def _bias_tiles_kernel(table_ref, o_ref, *, t, n_buckets):
    h = pl.program_id(0)
    c = pl.program_id(1)
    nb = n_buckets // 2
    max_exact = nb // 2
    kv = lax.broadcasted_iota(jnp.int32, (t, t), 0)
    q = lax.broadcasted_iota(jnp.int32, (t, t), 1)
    rel = (c - 2) * t + kv - q
    ret = jnp.where(rel > 0, nb, 0)
    n = jnp.abs(rel)
    large = max_exact + (jnp.log(jnp.maximum(n, 1).astype(F32) / max_exact)
                         / math.log(REL_MAX_DIST / max_exact) * (nb - max_exact)).astype(jnp.int32)
    large = jnp.minimum(large, nb - 1)
    bucket = ret + jnp.where(n < max_exact, n, large)
    out = jnp.zeros((t, t), F32)
    for b in range(n_buckets):
        out = jnp.where(bucket == b, table_ref[b, h], out)
    o_ref[...] = out * LOG2E


def _bias_tiles(rel_table, t):
    n_buckets, H = rel_table.shape
    return pl.pallas_call(
        functools.partial(_bias_tiles_kernel, t=t, n_buckets=n_buckets),
        grid=(H, 5),
        in_specs=[pl.BlockSpec(memory_space=pltpu.SMEM)],
        out_specs=pl.BlockSpec((None, None, t, t), lambda h, c: (h, c, 0, 0)),
        out_shape=jax.ShapeDtypeStruct((H, 5, t, t), F32),
        compiler_params=_cparams(("arbitrary", "arbitrary")),
        name="bias_tiles",
    )(rel_table)


def _attn_kernel(lam_ref, qt_ref, k_ref, vt_ref, bias_ref, g_ref, o_ref, acc1_ref, acc2_ref,
                 sa_ref, sb_ref, *, t, nk, hd, out_scale):
    qi = pl.program_id(2)
    dv = 2 * hd
    qt = qt_ref[...]
    half = lax.broadcasted_iota(jnp.int32, qt.shape, 0) < hd
    zero = jnp.zeros_like(qt)
    q1 = jnp.where(half, qt, zero)
    q2 = jnp.where(half, zero, qt)
    acc1_ref[...] = jnp.zeros_like(acc1_ref)
    acc2_ref[...] = jnp.zeros_like(acc2_ref)

    def scores(ki, s_ref):
        off = pl.multiple_of(ki * t, t)
        kb = k_ref[pl.ds(off, t), :]
        bias = bias_ref[jnp.clip(ki - qi, -2, 2) + 2]
        s1 = _dot(kb, q1) + bias
        s_ref[0] = s1
        s2 = _dot(kb, q2) + bias
        s_ref[1] = s2
        return jnp.max(s1, axis=0, keepdims=True), jnp.max(s2, axis=0, keepdims=True)

    def accumulate(ki, s_ref, mt, m):
        off = pl.multiple_of(ki * t, t)
        vb = vt_ref[:, pl.ds(off, t)]
        out = []
        for j, acc_ref in enumerate((acc1_ref, acc2_ref)):
            m_new = jnp.maximum(m[j], mt[j])
            alpha = jnp.exp2(m[j] - m_new)
            p = jnp.exp2(s_ref[j] - m_new)
            acc_ref[...] = alpha * acc_ref[...] + _dot(vb, p.astype(BF16))
            out.append(m_new)
        return tuple(out)

    def pair(j, carry):
        mt_a, m = carry
        ka = 2 * j
        mt_b = scores(ka + 1, sb_ref)
        m = accumulate(ka, sa_ref, mt_a, m)
        mt_a = scores(jnp.minimum(ka + 2, nk - 1), sa_ref)
        m = accumulate(ka + 1, sb_ref, mt_b, m)
        return mt_a, m

    neg = jnp.full((1, t), -jnp.inf, F32)
    mt0 = scores(0, sa_ref)
    _, (m1, m2) = lax.fori_loop(0, nk // 2, pair, (mt0, (neg, neg)))
    a1 = acc1_ref[...]
    a2 = acc2_ref[...]
    o = a1[:dv] / a1[dv:dv + 1] - lam_ref[0] * (a2[:dv] / a2[dv:dv + 1])
    ms = jnp.mean(o * o, axis=0, keepdims=True)
    o = o * lax.rsqrt(ms + LN_EPS) * g_ref[...] * out_scale
    o_ref[...] = o.T.astype(BF16)


def _attention(lam, qt, k, vt, bias, g_col, B, L, H, hd, t, lam_init):
    Da = H * 2 * hd
    nq = L // t
    assert nq % 2 == 0
    dvp = vt.shape[2]
    k3 = k.reshape(B, L, Da)
    out = pl.pallas_call(
        functools.partial(_attn_kernel, t=t, nk=nq, hd=hd, out_scale=1.0 - lam_init),
        grid=(B, H, nq),
        in_specs=[
            pl.BlockSpec(memory_space=pltpu.SMEM),
            pl.BlockSpec((None, 2 * hd, t), lambda b, h, q: (b, h, q)),
            pl.BlockSpec((None, L, 2 * hd), lambda b, h, q: (b, 0, h)),
            pl.BlockSpec((None, None, dvp, L), lambda b, h, q: (b, h, 0, 0)),
            pl.BlockSpec((None, 5, t, t), lambda b, h, q: (h, 0, 0, 0)),
            pl.BlockSpec((2 * hd, 1), lambda b, h, q: (0, 0)),
        ],
        out_specs=pl.BlockSpec((None, t, 2 * hd), lambda b, h, q: (b, q, h)),
        out_shape=jax.ShapeDtypeStruct((B, L, Da), BF16),
        scratch_shapes=[pltpu.VMEM((dvp, t), F32), pltpu.VMEM((dvp, t), F32),
                        pltpu.VMEM((2, t, t), F32), pltpu.VMEM((2, t, t), F32)],
        compiler_params=_cparams(("arbitrary", "arbitrary", "arbitrary")),
        name="diff_attn",
    )(lam, qt, k3, vt, bias, g_col)
    return out.reshape(B * L, Da)


def _proj_out_kernel(cy_ref, vx_ref, x0_ref, hb_ref, ya_ref, x_ref, woh_ref, woa_ref, g_ref, b_ref,
                     wrc_ref, wrh_ref, x1_ref, eid_ref, gate_ref, *, alpha, n_groups, per_group):
    yh = ((cy_ref[...] + vx_ref[...] * hb_ref[...]) * x0_ref[...].astype(F32)).astype(BF16)
    mix = _dot(yh, woh_ref[...]) + _dot(ya_ref[...], woa_ref[...])
    x1 = _layer_norm_rows(alpha * x_ref[...] + mix, g_ref[...], b_ref[...])
    x1_ref[...] = x1
    hi = x1.astype(BF16)
    lo = (x1 - hi.astype(F32)).astype(BF16)
    lg2 = _dot(hi, wrc_ref[...])
    lg = lg2[:, :LANES] + lg2[:, LANES:] + _dot(lo, wrh_ref[...])
    col = lax.broadcasted_iota(jnp.int32, lg.shape, 1).astype(F32)
    far = jnp.float32(LANES)
    neg_inf = jnp.float32(-jnp.inf)

    def first_col(mask):
        return jnp.min(jnp.where(mask, col, far), axis=-1, keepdims=True)

    gmask = col < n_groups
    glog = jnp.where(gmask, lg, neg_inf)
    gmax = jnp.max(glog, axis=-1, keepdims=True)
    g_sel = first_col(glog == gmax)
    g_w = 1.0 / jnp.sum(jnp.where(gmask, jnp.exp(glog - gmax), 0.0), axis=-1, keepdims=True)
    lo_col = n_groups + g_sel * per_group
    emask = (col >= lo_col) & (col < lo_col + per_group)
    elog = jnp.where(emask, lg, neg_inf)
    emax = jnp.max(elog, axis=-1, keepdims=True)
    eexp = jnp.where(emask, jnp.exp(elog - emax), 0.0)
    prob = eexp / jnp.sum(eexp, axis=-1, keepdims=True)
    p1 = jnp.max(prob, axis=-1, keepdims=True)
    i1 = first_col(emask & (prob == p1))
    mask2 = emask & (col != i1)
    p2 = jnp.max(jnp.where(mask2, prob, -1.0), axis=-1, keepdims=True)
    i2 = first_col(mask2 & (prob == p2))
    psum = p1 + p2
    eid = jnp.where(col == 0, i1 - n_groups, jnp.where(col == 1, i2 - n_groups, 0.0))
    eid_ref[...] = eid.astype(jnp.int32)
    gate_ref[...] = jnp.where(col == 0, g_w * p1 / psum, jnp.where(col == 1, g_w * p2 / psum, 0.0))


def _proj_out(cy, vx, x0c, hb, ya, x, woh, woa, g, b, wrc, wrh, alpha, n_groups, per_group, tm):
    T, D = x.shape
    Dh = cy.shape[1]
    Da = ya.shape[1]
    row = lambda i: (i, 0)
    fix = lambda i: (0, 0)
    return pl.pallas_call(
        functools.partial(_proj_out_kernel, alpha=alpha, n_groups=n_groups, per_group=per_group),
        grid=(T // tm,),
        in_specs=[
            pl.BlockSpec((tm, Dh), row),
            pl.BlockSpec((tm, Dh), row),
            pl.BlockSpec((tm, Dh), row),
            pl.BlockSpec((1, Dh), fix),
            pl.BlockSpec((tm, Da), row),
            pl.BlockSpec((tm, D), row),
            pl.BlockSpec((Dh, D), fix),
            pl.BlockSpec((Da, D), fix),
            pl.BlockSpec((1, D), fix),
            pl.BlockSpec((1, D), fix),
            pl.BlockSpec((D, 2 * LANES), fix),
            pl.BlockSpec((D, LANES), fix),
        ],
        out_specs=[
            pl.BlockSpec((tm, D), row),
            pl.BlockSpec((tm, LANES), row),
            pl.BlockSpec((tm, LANES), row),
        ],
        out_shape=[
            jax.ShapeDtypeStruct((T, D), F32),
            jax.ShapeDtypeStruct((T, LANES), jnp.int32),
            jax.ShapeDtypeStruct((T, LANES), F32),
        ],
        compiler_params=_cparams(("arbitrary",)),
        name="proj_out_ln_router",
    )(cy, vx, x0c, hb, ya, x, woh, woa, g, b, wrc, wrh)


def _moe_kernel(be_ref, nu_ref, src_ref, srcn_ref, dst_ref, x_hbm, wg_ref, wu_ref, wd_ref, out_hbm,
                xbuf, ybuf, wgb, wub, wdb, sem_in, sem_out, *, bm, chunk):
    i = pl.program_id(0)
    n_used = nu_ref[0]
    slot = i % 2
    other = 1 - slot

    def gather_row(idx_ref, s, r):
        return pltpu.make_async_copy(x_hbm.at[pl.ds(idx_ref[0, r], 1), :],
                                     xbuf.at[s, pl.ds(r, 1), :], sem_in.at[s])

    def scatter_row(s, r):
        return pltpu.make_async_copy(ybuf.at[s, pl.ds(r, 1), :],
                                     out_hbm.at[pl.ds(dst_ref[0, r], 1), :], sem_out.at[s])

    def wait_gather(s):
        pltpu.make_async_copy(x_hbm.at[pl.ds(0, bm), :], xbuf.at[s], sem_in.at[s]).wait()

    def wait_scatter(s):
        pltpu.make_async_copy(ybuf.at[s], out_hbm.at[pl.ds(0, bm), :], sem_out.at[s]).wait()

    @pl.when(i == 0)
    def _():
        def body(r, c):
            gather_row(src_ref, 0, r).start()
            return c

        lax.fori_loop(0, bm, body, 0, unroll=8)

    @pl.when(i < n_used)
    def _():
        changed = jnp.logical_or(i == 0, be_ref[i] != be_ref[jnp.maximum(i - 1, 0)])

        @pl.when(changed)
        def _():
            wgb[...] = wg_ref[...].astype(BF16)
            wub[...] = wu_ref[...].astype(BF16)
            wdb[...] = wd_ref[...].astype(BF16)

        wait_gather(slot)
        for r in range(bm):
            gather_row(srcn_ref, other, r).start()
        xb = xbuf[slot].astype(BF16)
        hg = _dot(xb, wgb[...])
        hu = _dot(xb, wub[...])
        act = ((hg / (1.0 + jnp.exp(-hg))) * hu).astype(BF16)
        for c in range(bm // chunk):
            rows = slice(c * chunk, (c + 1) * chunk)
            ybuf[slot, rows, :] = _dot(act[rows], wdb[...])
            for r in range(c * chunk, (c + 1) * chunk):
                scatter_row(slot, r).start()

        @pl.when(i > 0)
        def _():
            wait_scatter(other)

        @pl.when(i == n_used - 1)
        def _():
            wait_scatter(slot)
            wait_gather(other)


def _moe_experts(block_e, n_used, src, dst, x1, w_gate, w_up, w_down, n_rows_out, bm):
    n_blocks = block_e.shape[0]
    T, D = x1.shape
    E, _, De = w_gate.shape
    idx_block = (None, 1, bm)
    grid_spec = pltpu.PrefetchScalarGridSpec(
        num_scalar_prefetch=2,
        grid=(n_blocks,),
        in_specs=[
            pl.BlockSpec(idx_block, lambda i, be, nu: (i, 0, 0), memory_space=pltpu.SMEM),
            pl.BlockSpec(idx_block, lambda i, be, nu: (jnp.minimum(i + 1, n_blocks - 1), 0, 0),
                         memory_space=pltpu.SMEM),
            pl.BlockSpec(idx_block, lambda i, be, nu: (i, 0, 0), memory_space=pltpu.SMEM),
            pl.BlockSpec(memory_space=pl.ANY),
            pl.BlockSpec((None, D, De), lambda i, be, nu: (be[i], 0, 0)),
            pl.BlockSpec((None, D, De), lambda i, be, nu: (be[i], 0, 0)),
            pl.BlockSpec((None, De, D), lambda i, be, nu: (be[i], 0, 0)),
        ],
        out_specs=pl.BlockSpec(memory_space=pl.ANY),
        scratch_shapes=[
            pltpu.VMEM((2, bm, D), F32),
            pltpu.VMEM((2, bm, D), F32),
            pltpu.VMEM((D, De), BF16),
            pltpu.VMEM((D, De), BF16),
            pltpu.VMEM((De, D), BF16),
            pltpu.SemaphoreType.DMA((2,)),
            pltpu.SemaphoreType.DMA((2,)),
        ],
    )
    return pl.pallas_call(
        functools.partial(_moe_kernel, bm=bm, chunk=min(bm, 128)),
        grid_spec=grid_spec,
        out_shape=jax.ShapeDtypeStruct((n_rows_out, D), F32),
        compiler_params=_cparams(("arbitrary",)),
        name="moe_experts",
    )(block_e, n_used, src, src, dst, x1, w_gate, w_up, w_down)


def _combine_kernel(ya_ref, yb_ref, gate_ref, x1_ref, g_ref, b_ref, *o_refs, alpha, na):
    gates = gate_ref[...]
    y = gates[:, 0:1] * ya_ref[...] + gates[:, 1:2] * yb_ref[...]
    res = _layer_norm_rows(alpha * x1_ref[...] + y, g_ref[...], b_ref[...])
    if len(o_refs) == 1:
        o_refs[0][...] = res
    else:
        i = pl.program_id(0)

        @pl.when(i < na)
        def _():
            o_refs[0][...] = res

        @pl.when(i >= na)
        def _():
            o_refs[1][...] = res


def _combine_ln(y2, gates, x1, g, b, alpha, tm, split_rows=None):
    T, D = x1.shape
    nt = T // tm
    row = lambda i: (i, 0)
    fix = lambda i: (0, 0)
    if split_rows is None:
        na = nt
        out_specs = pl.BlockSpec((tm, D), row)
        out_shape = jax.ShapeDtypeStruct((T, D), F32)
    else:
        na = split_rows // tm
        out_specs = [pl.BlockSpec((tm, D), lambda i: (jnp.minimum(i, na - 1), 0)),
                     pl.BlockSpec((tm, D), lambda i: (jnp.maximum(i - na, 0), 0))]
        out_shape = [jax.ShapeDtypeStruct((split_rows, D), F32),
                     jax.ShapeDtypeStruct((T - split_rows, D), F32)]
    return pl.pallas_call(
        functools.partial(_combine_kernel, alpha=alpha, na=na),
        grid=(nt,),
        in_specs=[
            pl.BlockSpec((tm, D), row),
            pl.BlockSpec((tm, D), lambda i: (nt + i, 0)),
            pl.BlockSpec((tm, LANES), row),
            pl.BlockSpec((tm, D), row),
            pl.BlockSpec((1, D), fix),
            pl.BlockSpec((1, D), fix),
        ],
        out_specs=out_specs,
        out_shape=out_shape,
        compiler_params=_cparams(("arbitrary",)),
        name="moe_combine_ln",
    )(y2, y2, gates, x1, g, b)


def _hyena_filter(L, dh, f_w1, f_b1, f_freq, f_w2, f_b2, f_w3):
    emb = f_w1.shape[0]
    bands = (emb - 1) // 2
    t = jnp.linspace(0.0, 1.0, L, dtype=F32)[:, None]
    w = (2.0 * math.pi / L) * jnp.arange(L, dtype=F32)[:, None]
    f = jnp.linspace(1e-4, bands - 1, bands, dtype=F32)[None, :]
    z = jnp.concatenate([t, jnp.cos(f * w), -jnp.sin(f * w)], axis=-1)
    hp = lax.Precision.HIGHEST
    h = jnp.sin(f_freq * (jnp.dot(z, f_w1, precision=hp) + f_b1))
    h = jnp.sin(f_freq * (jnp.dot(h, f_w2, precision=hp) + f_b2))
    h = jnp.dot(h, f_w3, precision=hp)
    max_decay = math.log(DECAY_TARGET) / FAST_DECAY_PCT
    min_decay = math.log(DECAY_TARGET) / SLOW_DECAY_PCT
    deltas = jnp.abs(jnp.linspace(min_decay, max_decay, dh, dtype=F32))
    decay = jnp.exp(-t * deltas)
    h_fwd = h[:, :dh] * decay
    h_bwd = h[:, dh:] * decay
    k_full = jnp.concatenate([h_fwd, jnp.zeros((1, dh), F32), h_bwd[1:][::-1]], axis=0)
    return k_full / jnp.sum(jnp.abs(k_full), axis=0, keepdims=True)


def _dispatch_tables(eid, n_experts, bm):
    T = eid.shape[0]
    A = T * TOP_K
    flat_e = eid.reshape(A)
    order = jnp.argsort(flat_e, stable=True).astype(jnp.int32)
    se = flat_e[order]
    experts = jnp.arange(n_experts, dtype=jnp.int32)
    starts = jnp.searchsorted(se, experts, side="left").astype(jnp.int32)
    counts = jnp.searchsorted(se, experts, side="right").astype(jnp.int32) - starts
    pcounts = (counts + bm - 1) // bm * bm
    pends = jnp.cumsum(pcounts)
    pstarts = pends - pcounts
    n_blocks = -(-A // bm) + n_experts
    P = n_blocks * bm
    block_e = jnp.minimum(
        jnp.searchsorted(pends, jnp.arange(n_blocks, dtype=jnp.int32) * bm, side="right"),
        n_experts - 1).astype(jnp.int32)
    e_row = jnp.repeat(block_e, bm, total_repeat_length=P)
    j = jnp.arange(P, dtype=jnp.int32) - pstarts[e_row]
    pad = j >= counts[e_row]
    row_asg = jnp.where(pad, -1, order[jnp.clip(starts[e_row] + j, 0, A - 1)])
    src = jnp.maximum(row_asg, 0) // TOP_K
    dst = jnp.where(pad, A + jnp.cumsum(pad.astype(jnp.int32)) - 1, (row_asg % TOP_K) * T + row_asg // TOP_K)
    n_used = (pends[-1] // bm).astype(jnp.int32).reshape(1)
    return block_e, n_used, src.reshape(n_blocks, 1, bm), dst.astype(jnp.int32).reshape(n_blocks, 1, bm), P


def _pick(n, pref):
    t = min(n, pref)
    assert n % t == 0, (n, t)
    return t


def _trunk(xa, xb, p):
    Ba, L, D = xa.shape
    Bb = xb.shape[0]
    assert xb.shape[1:] == (L, D)
    B = Ba + Bb
    T = B * L
    depth = p["w_in"].shape[0]
    dh = p["h_bias"].shape[1]
    H = p["rel_table"].shape[1]
    hd = p["lam_q1"].shape[1]
    da = H * 2 * hd
    n_groups = p["w_route_group"].shape[2]
    n_experts = p["w_route_expert"].shape[2]
    per_group = n_experts // n_groups
    alpha = (2.0 * depth) ** 0.25
    assert L % FFT_N2 == 0 and (L // FFT_N2) % 8 == 0
    assert 2 * hd == LANES and dh % LANES == 0 and n_groups + n_experts <= LANES

    tm = _pick(L, 512)
    t_attn = _pick(L, 512)
    bm = 512

    xs = _ln_embed(xa.reshape(Ba * L, D), xb.reshape(Bb * L, D), p["ln_emb_g"], p["ln_emb_b"], tm)
    tables = _dft_tables(L)
    g_full, g_fwd, g_inv, f2, f2i = tables
    bias = _bias_tiles(p["rel_table"], t_attn)

    for l in range(depth):
        lam_init = 0.8 - 0.6 * math.exp(-0.3 * l)
        w_in = p["w_in"][l]
        wh = w_in[:, :3 * dh].astype(BF16)
        wa = w_in[:, 3 * dh:]
        wqt = (wa[:, :da] * (hd ** -0.5 * LOG2E)).T.astype(BF16)
        wk = wa[:, da:2 * da].astype(BF16)
        wvt = wa[:, 2 * da:].T.astype(BF16)
        uh, k, qt, vt = _proj_in(xs, wh, wk, wqt, wvt, B, L, H, tm)

        vx, x0c = _hyena_pre(uh, p["conv_w"][l], p["conv_b"][l], B, L, tm)
        k_full = _hyena_filter(L, dh, p["f_w1"][l], p["f_b1"][l], p["f_freq"][l], p["f_w2"][l],
                               p["f_b2"][l], p["f_w3"][l])
        h_spec = _filter_spec(k_full, g_full, f2, LANES)
        cy = _fft_conv(vx, g_fwd, g_inv, f2, f2i, h_spec, B, L, LANES)

        lam = (jnp.exp(jnp.sum(p["lam_q1"][l] * p["lam_k1"][l]))
               - jnp.exp(jnp.sum(p["lam_q2"][l] * p["lam_k2"][l])) + lam_init).reshape(1).astype(F32)
        ya = _attention(lam, qt, k, vt, bias, p["subln_g"][l].reshape(2 * hd, 1), B, L, H, hd, t_attn,
                        lam_init)

        w_out = p["w_out"][l]
        w_r = jnp.concatenate([p["w_route_group"][l], p["w_route_expert"][l]], axis=1)
        w_r = jnp.pad(w_r, ((0, 0), (0, LANES - w_r.shape[1])))
        w_r_hi = w_r.astype(BF16)
        w_r_lo = (w_r - w_r_hi.astype(F32)).astype(BF16)
        x1, eid, gates = _proj_out(
            cy, vx, x0c, p["h_bias"][l].reshape(1, dh), ya, xs,
            w_out[:dh].astype(BF16), w_out[dh:].astype(BF16),
            p["ln1_g"][l].reshape(1, D), p["ln1_b"][l].reshape(1, D),
            jnp.concatenate([w_r_hi, w_r_lo], axis=1), w_r_hi,
            alpha, n_groups, per_group, tm)

        block_e, n_used, src, dst, n_rows = _dispatch_tables(eid[:, :TOP_K], n_experts, bm)
        y2 = _moe_experts(block_e, n_used, src, dst, x1, p["w_gate"][l], p["w_up"][l], p["w_down"][l],
                          n_rows, bm)
        xs = _combine_ln(y2, gates, x1, p["ln2_g"][l].reshape(1, D), p["ln2_b"][l].reshape(1, D),
                         alpha, tm, split_rows=Ba * L if l == depth - 1 else None)
    return xs[0].reshape(Ba, L, D), xs[1].reshape(Bb, L, D)


def kernel(x_prompt, x_sample, ln_emb_g, ln_emb_b, rel_table, w_in, conv_w, conv_b, f_w1, f_b1, f_freq,
           f_w2, f_b2, f_w3, h_bias, lam_q1, lam_k1, lam_q2, lam_k2, subln_g, w_out, ln1_g, ln1_b,
           w_route_group, w_route_expert, w_gate, w_up, w_down, ln2_g, ln2_b):
    p = dict(ln_emb_g=ln_emb_g, ln_emb_b=ln_emb_b, rel_table=rel_table, w_in=w_in, conv_w=conv_w,
             conv_b=conv_b, f_w1=f_w1, f_b1=f_b1, f_freq=f_freq, f_w2=f_w2, f_b2=f_b2, f_w3=f_w3,
             h_bias=h_bias, lam_q1=lam_q1, lam_k1=lam_k1, lam_q2=lam_q2, lam_k2=lam_k2,
             subln_g=subln_g, w_out=w_out, ln1_g=ln1_g, ln1_b=ln1_b, w_route_group=w_route_group,
             w_route_expert=w_route_expert, w_gate=w_gate, w_up=w_up, w_down=w_down, ln2_g=ln2_g,
             ln2_b=ln2_b)
    return _trunk(x_prompt, x_sample, p)
```

```python
import functools
import math

import jax
import jax.numpy as jnp
from jax import lax
from jax.experimental import pallas as pl
from jax.experimental.pallas import tpu as pltpu

F32 = jnp.float32
BF16 = jnp.bfloat16

LN_EPS = 1e-5
REL_MAX_DIST = 128
TOP_K = 2
DECAY_TARGET = 1e-2
FAST_DECAY_PCT = 0.3
SLOW_DECAY_PCT = 1.5

LANES = 128
BF16_SUBLANES = 16
LOG2E = 1.4426950408889634
FFT_N2 = 128
VMEM_LIMIT_BYTES = 56 * 1024 * 1024


def _cparams(sem):
    return pltpu.CompilerParams(dimension_semantics=sem, vmem_limit_bytes=VMEM_LIMIT_BYTES)


def _dot(a, b):
    return jnp.dot(a, b, preferred_element_type=F32)


def _layer_norm_rows(z, g, b):
    mu = jnp.mean(z, axis=-1, keepdims=True)
    zc = z - mu
    var = jnp.mean(zc * zc, axis=-1, keepdims=True)
    return zc * lax.rsqrt(var + LN_EPS) * g + b


def _ln_kernel(xa_ref, xb_ref, g_ref, b_ref, o_ref, *, na):
    i = pl.program_id(0)

    @pl.when(i < na)
    def _():
        o_ref[...] = _layer_norm_rows(xa_ref[...], g_ref[...], b_ref[...])

    @pl.when(i >= na)
    def _():
        o_ref[...] = _layer_norm_rows(xb_ref[...], g_ref[...], b_ref[...])


def _ln_embed(xa, xb, g, b, tm):
    Ta, D = xa.shape
    Tb = xb.shape[0]
    na = Ta // tm
    nb = Tb // tm
    return pl.pallas_call(
        functools.partial(_ln_kernel, na=na),
        grid=(na + nb,),
        in_specs=[
            pl.BlockSpec((tm, D), lambda i: (jnp.minimum(i, na - 1), 0)),
            pl.BlockSpec((tm, D), lambda i: (jnp.maximum(i - na, 0), 0)),
            pl.BlockSpec((1, D), lambda i: (0, 0)),
            pl.BlockSpec((1, D), lambda i: (0, 0)),
        ],
        out_specs=pl.BlockSpec((tm, D), lambda i: (i, 0)),
        out_shape=jax.ShapeDtypeStruct((Ta + Tb, D), F32),
        compiler_params=_cparams(("arbitrary",)),
        name="ln_embed",
    )(xa, xb, g.reshape(1, D), b.reshape(1, D))


def _proj_in_kernel(x_ref, wh_ref, wk_ref, wqt_ref, wvt_ref, uh_ref, k_ref, qt_ref, vt_ref):
    xb = x_ref[...].astype(BF16)
    uh_ref[...] = _dot(xb, wh_ref[...]).astype(BF16)
    k_ref[...] = _dot(xb, wk_ref[...]).astype(BF16)
    nt = (((1,), (1,)), ((), ()))
    qt_ref[...] = lax.dot_general(wqt_ref[...], xb, nt, preferred_element_type=F32).astype(BF16)
    vt = lax.dot_general(wvt_ref[...], xb, nt, preferred_element_type=F32).astype(BF16)
    n_heads, dvp, tm = vt_ref.shape
    dv = vt.shape[0] // n_heads
    pad_rows = lax.broadcasted_iota(jnp.int32, (dvp - dv, tm), 0)
    ones_pad = jnp.where(pad_rows == 0, 1.0, 0.0).astype(BF16)
    for h in range(n_heads):
        vt_ref[h, 0:dv, :] = vt[h * dv:(h + 1) * dv]
        vt_ref[h, dv:dvp, :] = ones_pad


def _proj_in(x2d, wh, wk, wqt, wvt, B, L, H, tm):
    T, D = x2d.shape
    Ch = wh.shape[1]
    Da = wk.shape[1]
    nl = L // tm
    dvp = Da // H + BF16_SUBLANES
    return pl.pallas_call(
        _proj_in_kernel,
        grid=(B, nl),
        in_specs=[
            pl.BlockSpec((tm, D), lambda b, i: (b * nl + i, 0)),
            pl.BlockSpec((D, Ch), lambda b, i: (0, 0)),
            pl.BlockSpec((D, Da), lambda b, i: (0, 0)),
            pl.BlockSpec((Da, D), lambda b, i: (0, 0)),
            pl.BlockSpec((Da, D), lambda b, i: (0, 0)),
        ],
        out_specs=[
            pl.BlockSpec((tm, Ch), lambda b, i: (b * nl + i, 0)),
            pl.BlockSpec((tm, Da), lambda b, i: (b * nl + i, 0)),
            pl.BlockSpec((None, Da, tm), lambda b, i: (b, 0, i)),
            pl.BlockSpec((None, H, dvp, tm), lambda b, i: (b, 0, 0, i)),
        ],
        out_shape=[
            jax.ShapeDtypeStruct((T, Ch), BF16),
            jax.ShapeDtypeStruct((T, Da), BF16),
            jax.ShapeDtypeStruct((B, Da, L), BF16),
            jax.ShapeDtypeStruct((B, H, dvp, L), BF16),
        ],
        compiler_params=_cparams(("arbitrary", "arbitrary")),
        name="proj_in",
    )(x2d, wh, wk, wqt, wvt)


def _hyena_pre_kernel(u_ref, up_ref, un_ref, w_ref, b_ref, vx_ref, x0_ref, *, dh, halo):
    i = pl.program_id(1)
    last = pl.num_programs(1) - 1
    u = u_ref[...].astype(F32)
    tm = u.shape[0]
    prev_row = up_ref[halo - 1:halo, :].astype(F32)
    next_row = un_ref[0:1, :].astype(F32)
    prev_row = jnp.where(i == 0, 0.0, prev_row)
    next_row = jnp.where(i == last, 0.0, next_row)
    rows = lax.broadcasted_iota(jnp.int32, (tm, 1), 0)
    um1 = jnp.where(rows == 0, prev_row, pltpu.roll(u, 1, axis=0))
    up1 = jnp.where(rows == tm - 1, next_row, pltpu.roll(u, tm - 1, axis=0))
    w = w_ref[...]
    y = b_ref[...] + um1 * w[0:1, :]
    y = y + u * w[1:2, :]
    y = y + up1 * w[2:3, :]
    x0_ref[...] = y[:, :dh].astype(BF16)
    vx_ref[...] = y[:, 2 * dh:] * y[:, dh:2 * dh]


def _hyena_pre(uh, conv_w, conv_b, B, L, tm):
    T, Ch = uh.shape
    dh = Ch // 3
    halo = 16
    nl = L // tm
    r = tm // halo
    nhalo = T // halo
    return pl.pallas_call(
        functools.partial(_hyena_pre_kernel, dh=dh, halo=halo),
        grid=(B, nl),
        in_specs=[
            pl.BlockSpec((tm, Ch), lambda b, i: (b * nl + i, 0)),
            pl.BlockSpec((halo, Ch), lambda b, i: (jnp.maximum((b * nl + i) * r - 1, 0), 0)),
            pl.BlockSpec((halo, Ch), lambda b, i: (jnp.minimum((b * nl + i + 1) * r, nhalo - 1), 0)),
            pl.BlockSpec((3, Ch), lambda b, i: (0, 0)),
            pl.BlockSpec((1, Ch), lambda b, i: (0, 0)),
        ],
        out_specs=[
            pl.BlockSpec((tm, dh), lambda b, i: (b * nl + i, 0)),
            pl.BlockSpec((tm, dh), lambda b, i: (b * nl + i, 0)),
        ],
        out_shape=[
            jax.ShapeDtypeStruct((T, dh), F32),
            jax.ShapeDtypeStruct((T, dh), BF16),
        ],
        compiler_params=_cparams(("arbitrary", "arbitrary")),
        name="hyena_pre",
    )(uh, uh, uh, conv_w, conv_b.reshape(1, Ch))


def _dft_tables(L):
    N = 2 * L
    N1 = N // FFT_N2
    NB = L // FFT_N2
    r1 = jnp.arange(N1, dtype=jnp.int32)
    r2 = jnp.arange(FFT_N2, dtype=jnp.int32)

    def cos_sin(prod, period):
        ang = (prod % period).astype(F32) * (2.0 * math.pi / period)
        return jnp.cos(ang), jnp.sin(ang)

    c1, s1 = cos_sin(r1[:, None] * r1[None, :], N1)
    f1 = jnp.concatenate([c1, -s1], axis=0).astype(BF16)
    f1i = jnp.concatenate([c1[:NB], -s1[:NB]], axis=1).astype(BF16)
    ct, st = cos_sin(r1[:, None] * r2[None, :], N)
    c2, s2 = cos_sin(r2[:, None] * r2[None, :], FFT_N2)
    f2 = jnp.concatenate([c2, -s2], axis=1).astype(BF16)
    f2i = jnp.concatenate([jnp.concatenate([c2, s2], 1),
                           jnp.concatenate([-s2, c2], 1)], 0).astype(BF16)
    return dict(f1=f1, f1d=f1[:, :NB], f1i=f1i, tr=ct, ti=-st, f2=f2, f2i=f2i)


FFT_GROUP = 8


def _dft2_forward(xs, f1_ref, tr, ti, f2_ref):
    n1 = tr.shape[0]
    a = _dot(f1_ref[...], jnp.concatenate(xs, axis=1))
    rows = []
    for j in range(len(xs)):
        aj = a[:, j * FFT_N2:(j + 1) * FFT_N2]
        ar, ai = aj[:n1], aj[n1:]
        rows += [ar * tr - ai * ti, ar * ti + ai * tr]
    p = _dot(jnp.concatenate(rows, axis=0).astype(BF16), f2_ref[...])
    out = []
    for j in range(len(xs)):
        pr, pi = p[2 * j * n1:(2 * j + 1) * n1], p[(2 * j + 1) * n1:(2 * j + 2) * n1]
        out.append((pr[:, :FFT_N2] - pi[:, FFT_N2:], pr[:, FFT_N2:] + pi[:, :FFT_N2]))
    return out


def _filter_spec_kernel(kf_ref, f1_ref, tr_ref, ti_ref, f2_ref, h_ref, *, inv_n, unroll):
    tr, ti = tr_ref[...], ti_ref[...]

    def group(g, carry):
        c0 = g * FFT_GROUP
        xs = [kf_ref[c0 + j].astype(BF16) for j in range(FFT_GROUP)]
        for j, (sr, si) in enumerate(_dft2_forward(xs, f1_ref, tr, ti, f2_ref)):
            h_ref[c0 + j] = (jnp.concatenate([sr, si], axis=1) * inv_n).astype(BF16)
        return carry

    lax.fori_loop(0, kf_ref.shape[0] // FFT_GROUP, group, 0, unroll=unroll)


def _filter_spec(kf, tb, cc):
    Dh, n1, _ = kf.shape
    fix2 = lambda c: (0, 0)
    return pl.pallas_call(
        functools.partial(_filter_spec_kernel, inv_n=1.0 / (n1 * FFT_N2), unroll=2),
        grid=(Dh // cc,),
        in_specs=[
            pl.BlockSpec((cc, n1, FFT_N2), lambda c: (c, 0, 0)),
            pl.BlockSpec((2 * n1, n1), fix2),
            pl.BlockSpec((n1, FFT_N2), fix2),
            pl.BlockSpec((n1, FFT_N2), fix2),
            pl.BlockSpec((FFT_N2, 2 * FFT_N2), fix2),
        ],
        out_specs=pl.BlockSpec((cc, n1, 2 * FFT_N2), lambda c: (c, 0, 0)),
        out_shape=jax.ShapeDtypeStruct((Dh, n1, 2 * FFT_N2), BF16),
        compiler_params=_cparams(("arbitrary",)),
        name="filter_spec",
    )(kf, tb["f1"], tb["tr"], tb["ti"], tb["f2"])


def _fft_conv_kernel(x_ref, f1_ref, f1i_ref, tr_ref, ti_ref, f2_ref, f2i_ref, h_ref, o_ref, *, unroll):
    tr, ti = tr_ref[...], ti_ref[...]
    n1 = tr.shape[0]

    def group(g, carry):
        c0 = g * FFT_GROUP
        xs = [x_ref[c0 + j].astype(BF16) for j in range(FFT_GROUP)]
        ys = []
        for j, (sr, si) in enumerate(_dft2_forward(xs, f1_ref, tr, ti, f2_ref)):
            h = h_ref[c0 + j].astype(F32)
            hr, hi = h[:, :FFT_N2], h[:, FFT_N2:]
            ys.append(jnp.concatenate([sr * hr - si * hi, sr * hi + si * hr], axis=1))
        b = _dot(jnp.concatenate(ys, axis=0).astype(BF16), f2i_ref[...])
        cols = []
        for j in range(FFT_GROUP):
            bj = b[j * n1:(j + 1) * n1]
            br, bi = bj[:, :FFT_N2], bj[:, FFT_N2:]
            cols.append(jnp.concatenate([br * tr + bi * ti, bi * tr - br * ti], axis=0))
        y = _dot(f1i_ref[...], jnp.concatenate(cols, axis=1).astype(BF16))
        for j in range(FFT_GROUP):
            o_ref[c0 + j] = y[:, j * FFT_N2:(j + 1) * FFT_N2]
        return carry

    lax.fori_loop(0, x_ref.shape[0] // FFT_GROUP, group, 0, unroll=unroll)


def _fft_conv(x4, tb, h_spec, cc):
    B, Dh, nb, _ = x4.shape
    n1 = 2 * nb
    fix2 = lambda c, b: (0, 0)
    return pl.pallas_call(
        functools.partial(_fft_conv_kernel, unroll=2),
        grid=(Dh // cc, B),
        in_specs=[
            pl.BlockSpec((None, cc, nb, FFT_N2), lambda c, b: (b, c, 0, 0)),
            pl.BlockSpec((2 * n1, nb), fix2),
            pl.BlockSpec((nb, 2 * n1), fix2),
            pl.BlockSpec((n1, FFT_N2), fix2),
            pl.BlockSpec((n1, FFT_N2), fix2),
            pl.BlockSpec((FFT_N2, 2 * FFT_N2), fix2),
            pl.BlockSpec((2 * FFT_N2, 2 * FFT_N2), fix2),
            pl.BlockSpec((cc, n1, 2 * FFT_N2), lambda c, b: (c, 0, 0)),
        ],
        out_specs=pl.BlockSpec((None, cc, nb, FFT_N2), lambda c, b: (b, c, 0, 0)),
        out_shape=jax.ShapeDtypeStruct((B, Dh, nb, FFT_N2), F32),
        compiler_params=_cparams(("arbitrary", "arbitrary")),
        name="fft_conv",
    )(x4, tb["f1d"], tb["f1i"], tb["tr"], tb["ti"], tb["f2"], tb["f2i"], h_spec)


def _bias_tiles_kernel(table_ref, o_ref, *, t, n_buckets):
    h = pl.program_id(0)
    c = pl.program_id(1)
    nb = n_buckets // 2
    max_exact = nb // 2
    kv = lax.broadcasted_iota(jnp.int32, (t, t), 0)
    q = lax.broadcasted_iota(jnp.int32, (t, t), 1)
    rel = (c - 2) * t + kv - q
    ret = jnp.where(rel > 0, nb, 0)
    n = jnp.abs(rel)
    large = max_exact + (jnp.log(jnp.maximum(n, 1).astype(F32) / max_exact)
                         / math.log(REL_MAX_DIST / max_exact) * (nb - max_exact)).astype(jnp.int32)
    large = jnp.minimum(large, nb - 1)
    bucket = ret + jnp.where(n < max_exact, n, large)
    out = jnp.zeros((t, t), F32)
    for b in range(n_buckets):
        out = jnp.where(bucket == b, table_ref[b, h], out)
    o_ref[...] = out * LOG2E


def _bias_tiles(rel_table, t):
    n_buckets, H = rel_table.shape
    return pl.pallas_call(
        functools.partial(_bias_tiles_kernel, t=t, n_buckets=n_buckets),
        grid=(H, 5),
        in_specs=[pl.BlockSpec(memory_space=pltpu.SMEM)],
        out_specs=pl.BlockSpec((None, None, t, t), lambda h, c: (h, c, 0, 0)),
        out_shape=jax.ShapeDtypeStruct((H, 5, t, t), F32),
        compiler_params=_cparams(("arbitrary", "arbitrary")),
        name="bias_tiles",
    )(rel_table)


def _attn_kernel(lam_ref, qt_ref, k_ref, vt_ref, bias_ref, g_ref, o_ref, acc1_ref, acc2_ref,
                 sa_ref, sb_ref, *, t, nk, hd, out_scale):
    qi = pl.program_id(2)
    dv = 2 * hd
    qt = qt_ref[...]
    half = lax.broadcasted_iota(jnp.int32, qt.shape, 0) < hd
    zero = jnp.zeros_like(qt)
    q1 = jnp.where(half, qt, zero)
    q2 = jnp.where(half, zero, qt)
    acc1_ref[...] = jnp.zeros_like(acc1_ref)
    acc2_ref[...] = jnp.zeros_like(acc2_ref)

    def scores(ki, s_ref):
        off = pl.multiple_of(ki * t, t)
        kb = k_ref[pl.ds(off, t), :]
        bias = bias_ref[jnp.clip(ki - qi, -2, 2) + 2]
        s1 = _dot(kb, q1) + bias
        s_ref[0] = s1
        s2 = _dot(kb, q2) + bias
        s_ref[1] = s2
        return jnp.max(s1, axis=0, keepdims=True), jnp.max(s2, axis=0, keepdims=True)

    def accumulate(ki, s_ref, mt, m):
        off = pl.multiple_of(ki * t, t)
        vb = vt_ref[:, pl.ds(off, t)]
        out = []
        for j, acc_ref in enumerate((acc1_ref, acc2_ref)):
            m_new = jnp.maximum(m[j], mt[j])
            alpha = jnp.exp2(m[j] - m_new)
            p = jnp.exp2(s_ref[j] - m_new)
            acc_ref[...] = alpha * acc_ref[...] + _dot(vb, p.astype(BF16))
            out.append(m_new)
        return tuple(out)

    def pair(j, carry):
        mt_a, m = carry
        ka = 2 * j
        mt_b = scores(ka + 1, sb_ref)
        m = accumulate(ka, sa_ref, mt_a, m)
        mt_a = scores(jnp.minimum(ka + 2, nk - 1), sa_ref)
        m = accumulate(ka + 1, sb_ref, mt_b, m)
        return mt_a, m

    neg = jnp.full((1, t), -jnp.inf, F32)
    mt0 = scores(0, sa_ref)
    _, (m1, m2) = lax.fori_loop(0, nk // 2, pair, (mt0, (neg, neg)))
    a1 = acc1_ref[...]
    a2 = acc2_ref[...]
    o = a1[:dv] / a1[dv:dv + 1] - lam_ref[0] * (a2[:dv] / a2[dv:dv + 1])
    ms = jnp.mean(o * o, axis=0, keepdims=True)
    o = o * lax.rsqrt(ms + LN_EPS) * g_ref[...] * out_scale
    o_ref[...] = o.T.astype(BF16)


def _attention(lam, qt, k, vt, bias, g_col, B, L, H, hd, t, lam_init):
    Da = H * 2 * hd
    nq = L // t
    assert nq % 2 == 0
    dvp = vt.shape[2]
    k3 = k.reshape(B, L, Da)
    out = pl.pallas_call(
        functools.partial(_attn_kernel, t=t, nk=nq, hd=hd, out_scale=1.0 - lam_init),
        grid=(B, H, nq),
        in_specs=[
            pl.BlockSpec(memory_space=pltpu.SMEM),
            pl.BlockSpec((None, 2 * hd, t), lambda b, h, q: (b, h, q)),
            pl.BlockSpec((None, L, 2 * hd), lambda b, h, q: (b, 0, h)),
            pl.BlockSpec((None, None, dvp, L), lambda b, h, q: (b, h, 0, 0)),
            pl.BlockSpec((None, 5, t, t), lambda b, h, q: (h, 0, 0, 0)),
            pl.BlockSpec((2 * hd, 1), lambda b, h, q: (0, 0)),
        ],
        out_specs=pl.BlockSpec((None, t, 2 * hd), lambda b, h, q: (b, q, h)),
        out_shape=jax.ShapeDtypeStruct((B, L, Da), BF16),
        scratch_shapes=[pltpu.VMEM((dvp, t), F32), pltpu.VMEM((dvp, t), F32),
                        pltpu.VMEM((2, t, t), F32), pltpu.VMEM((2, t, t), F32)],
        compiler_params=_cparams(("arbitrary", "arbitrary", "arbitrary")),
        name="diff_attn",
    )(lam, qt, k3, vt, bias, g_col)
    return out.reshape(B * L, Da)


def _proj_out_kernel(cy_ref, vx_ref, x0_ref, hb_ref, ya_ref, x_ref, woh_ref, woa_ref, g_ref, b_ref,
                     wrc_ref, wrh_ref, x1_ref, eid_ref, gate_ref, *, alpha, n_groups, per_group):
    yh = ((cy_ref[...] + vx_ref[...] * hb_ref[...]) * x0_ref[...].astype(F32)).astype(BF16)
    mix = _dot(yh, woh_ref[...]) + _dot(ya_ref[...], woa_ref[...])
    x1 = _layer_norm_rows(alpha * x_ref[...] + mix, g_ref[...], b_ref[...])
    x1_ref[...] = x1
    hi = x1.astype(BF16)
    lo = (x1 - hi.astype(F32)).astype(BF16)
    lg2 = _dot(hi, wrc_ref[...])
    lg = lg2[:, :LANES] + lg2[:, LANES:] + _dot(lo, wrh_ref[...])
    col = lax.broadcasted_iota(jnp.int32, lg.shape, 1).astype(F32)
    far = jnp.float32(LANES)
    neg_inf = jnp.float32(-jnp.inf)

    def first_col(mask):
        return jnp.min(jnp.where(mask, col, far), axis=-1, keepdims=True)

    gmask = col < n_groups
    glog = jnp.where(gmask, lg, neg_inf)
    gmax = jnp.max(glog, axis=-1, keepdims=True)
    g_sel = first_col(glog == gmax)
    g_w = 1.0 / jnp.sum(jnp.where(gmask, jnp.exp(glog - gmax), 0.0), axis=-1, keepdims=True)
    lo_col = n_groups + g_sel * per_group
    emask = (col >= lo_col) & (col < lo_col + per_group)
    elog = jnp.where(emask, lg, neg_inf)
    emax = jnp.max(elog, axis=-1, keepdims=True)
    eexp = jnp.where(emask, jnp.exp(elog - emax), 0.0)
    prob = eexp / jnp.sum(eexp, axis=-1, keepdims=True)
    p1 = jnp.max(prob, axis=-1, keepdims=True)
    i1 = first_col(emask & (prob == p1))
    mask2 = emask & (col != i1)
    p2 = jnp.max(jnp.where(mask2, prob, -1.0), axis=-1, keepdims=True)
    i2 = first_col(mask2 & (prob == p2))
    psum = p1 + p2
    eid = jnp.where(col == 0, i1 - n_groups, jnp.where(col == 1, i2 - n_groups, 0.0))
    eid_ref[...] = eid.astype(jnp.int32)
    gate_ref[...] = jnp.where(col == 0, g_w * p1 / psum, jnp.where(col == 1, g_w * p2 / psum, 0.0))


def _proj_out(cy, vx, x0c, hb, ya, x, woh, woa, g, b, wrc, wrh, alpha, n_groups, per_group, tm):
    T, D = x.shape
    Dh = cy.shape[1]
    Da = ya.shape[1]
    row = lambda i: (i, 0)
    fix = lambda i: (0, 0)
    return pl.pallas_call(
        functools.partial(_proj_out_kernel, alpha=alpha, n_groups=n_groups, per_group=per_group),
        grid=(T // tm,),
        in_specs=[
            pl.BlockSpec((tm, Dh), row),
            pl.BlockSpec((tm, Dh), row),
            pl.BlockSpec((tm, Dh), row),
            pl.BlockSpec((1, Dh), fix),
            pl.BlockSpec((tm, Da), row),
            pl.BlockSpec((tm, D), row),
            pl.BlockSpec((Dh, D), fix),
            pl.BlockSpec((Da, D), fix),
            pl.BlockSpec((1, D), fix),
            pl.BlockSpec((1, D), fix),
            pl.BlockSpec((D, 2 * LANES), fix),
            pl.BlockSpec((D, LANES), fix),
        ],
        out_specs=[
            pl.BlockSpec((tm, D), row),
            pl.BlockSpec((tm, LANES), row),
            pl.BlockSpec((tm, LANES), row),
        ],
        out_shape=[
            jax.ShapeDtypeStruct((T, D), F32),
            jax.ShapeDtypeStruct((T, LANES), jnp.int32),
            jax.ShapeDtypeStruct((T, LANES), F32),
        ],
        compiler_params=_cparams(("arbitrary",)),
        name="proj_out_ln_router",
    )(cy, vx, x0c, hb, ya, x, woh, woa, g, b, wrc, wrh)


def _moe_kernel(be_ref, nu_ref, src_ref, srcn_ref, dst_ref, x_hbm, wg_ref, wu_ref, wd_ref, out_hbm,
                xbuf, ybuf, wgb, wub, wdb, sem_in, sem_out, *, bm, chunk):
    i = pl.program_id(0)
    n_used = nu_ref[0]
    slot = i % 2
    other = 1 - slot

    def gather_row(idx_ref, s, r):
        return pltpu.make_async_copy(x_hbm.at[pl.ds(idx_ref[0, r], 1), :],
                                     xbuf.at[s, pl.ds(r, 1), :], sem_in.at[s])

    def scatter_row(s, r):
        return pltpu.make_async_copy(ybuf.at[s, pl.ds(r, 1), :],
                                     out_hbm.at[pl.ds(dst_ref[0, r], 1), :], sem_out.at[s])

    def wait_gather(s):
        pltpu.make_async_copy(x_hbm.at[pl.ds(0, bm), :], xbuf.at[s], sem_in.at[s]).wait()

    def wait_scatter(s):
        pltpu.make_async_copy(ybuf.at[s], out_hbm.at[pl.ds(0, bm), :], sem_out.at[s]).wait()

    @pl.when(i == 0)
    def _():
        def body(r, c):
            gather_row(src_ref, 0, r).start()
            return c

        lax.fori_loop(0, bm, body, 0, unroll=8)

    @pl.when(i < n_used)
    def _():
        changed = jnp.logical_or(i == 0, be_ref[i] != be_ref[jnp.maximum(i - 1, 0)])

        @pl.when(changed)
        def _():
            wgb[...] = wg_ref[...].astype(BF16)
            wub[...] = wu_ref[...].astype(BF16)
            wdb[...] = wd_ref[...].astype(BF16)

        wait_gather(slot)
        for r in range(bm):
            gather_row(srcn_ref, other, r).start()
        xb = xbuf[slot].astype(BF16)
        hg = _dot(xb, wgb[...])
        hu = _dot(xb, wub[...])
        act = ((hg / (1.0 + jnp.exp(-hg))) * hu).astype(BF16)
        for c in range(bm // chunk):
            rows = slice(c * chunk, (c + 1) * chunk)
            ybuf[slot, rows, :] = _dot(act[rows], wdb[...])
            for r in range(c * chunk, (c + 1) * chunk):
                scatter_row(slot, r).start()

        @pl.when(i > 0)
        def _():
            wait_scatter(other)

        @pl.when(i == n_used - 1)
        def _():
            wait_scatter(slot)
            wait_gather(other)


def _moe_experts(block_e, n_used, src, dst, x1, w_gate, w_up, w_down, n_rows_out, bm):
    n_blocks = block_e.shape[0]
    T, D = x1.shape
    E, _, De = w_gate.shape
    idx_block = (None, 1, bm)
    grid_spec = pltpu.PrefetchScalarGridSpec(
        num_scalar_prefetch=2,
        grid=(n_blocks,),
        in_specs=[
            pl.BlockSpec(idx_block, lambda i, be, nu: (i, 0, 0), memory_space=pltpu.SMEM),
            pl.BlockSpec(idx_block, lambda i, be, nu: (jnp.minimum(i + 1, n_blocks - 1), 0, 0),
                         memory_space=pltpu.SMEM),
            pl.BlockSpec(idx_block, lambda i, be, nu: (i, 0, 0), memory_space=pltpu.SMEM),
            pl.BlockSpec(memory_space=pl.ANY),
            pl.BlockSpec((None, D, De), lambda i, be, nu: (be[i], 0, 0)),
            pl.BlockSpec((None, D, De), lambda i, be, nu: (be[i], 0, 0)),
            pl.BlockSpec((None, De, D), lambda i, be, nu: (be[i], 0, 0)),
        ],
        out_specs=pl.BlockSpec(memory_space=pl.ANY),
        scratch_shapes=[
            pltpu.VMEM((2, bm, D), F32),
            pltpu.VMEM((2, bm, D), F32),
            pltpu.VMEM((D, De), BF16),
            pltpu.VMEM((D, De), BF16),
            pltpu.VMEM((De, D), BF16),
            pltpu.SemaphoreType.DMA((2,)),
            pltpu.SemaphoreType.DMA((2,)),
        ],
    )
    return pl.pallas_call(
        functools.partial(_moe_kernel, bm=bm, chunk=min(bm, 128)),
        grid_spec=grid_spec,
        out_shape=jax.ShapeDtypeStruct((n_rows_out, D), F32),
        compiler_params=_cparams(("arbitrary",)),
        name="moe_experts",
    )(block_e, n_used, src, src, dst, x1, w_gate, w_up, w_down)


def _combine_kernel(ya_ref, yb_ref, gate_ref, x1_ref, g_ref, b_ref, *o_refs, alpha, na):
    gates = gate_ref[...]
    y = gates[:, 0:1] * ya_ref[...] + gates[:, 1:2] * yb_ref[...]
    res = _layer_norm_rows(alpha * x1_ref[...] + y, g_ref[...], b_ref[...])
    if len(o_refs) == 1:
        o_refs[0][...] = res
    else:
        i = pl.program_id(0)

        @pl.when(i < na)
        def _():
            o_refs[0][...] = res

        @pl.when(i >= na)
        def _():
            o_refs[1][...] = res


def _combine_ln(y2, gates, x1, g, b, alpha, tm, split_rows=None):
    T, D = x1.shape
    nt = T // tm
    row = lambda i: (i, 0)
    fix = lambda i: (0, 0)
    if split_rows is None:
        na = nt
        out_specs = pl.BlockSpec((tm, D), row)
        out_shape = jax.ShapeDtypeStruct((T, D), F32)
    else:
        na = split_rows // tm
        out_specs = [pl.BlockSpec((tm, D), lambda i: (jnp.minimum(i, na - 1), 0)),
                     pl.BlockSpec((tm, D), lambda i: (jnp.maximum(i - na, 0), 0))]
        out_shape = [jax.ShapeDtypeStruct((split_rows, D), F32),
                     jax.ShapeDtypeStruct((T - split_rows, D), F32)]
    return pl.pallas_call(
        functools.partial(_combine_kernel, alpha=alpha, na=na),
        grid=(nt,),
        in_specs=[
            pl.BlockSpec((tm, D), row),
            pl.BlockSpec((tm, D), lambda i: (nt + i, 0)),
            pl.BlockSpec((tm, LANES), row),
            pl.BlockSpec((tm, D), row),
            pl.BlockSpec((1, D), fix),
            pl.BlockSpec((1, D), fix),
        ],
        out_specs=out_specs,
        out_shape=out_shape,
        compiler_params=_cparams(("arbitrary",)),
        name="moe_combine_ln",
    )(y2, y2, gates, x1, g, b)


def _hyena_filter(L, dh, f_w1, f_b1, f_freq, f_w2, f_b2, f_w3):
    emb = f_w1.shape[0]
    bands = (emb - 1) // 2
    t = jnp.linspace(0.0, 1.0, L, dtype=F32)[:, None]
    w = (2.0 * math.pi / L) * jnp.arange(L, dtype=F32)[:, None]
    f = jnp.linspace(1e-4, bands - 1, bands, dtype=F32)[None, :]
    z = jnp.concatenate([t, jnp.cos(f * w), -jnp.sin(f * w)], axis=-1)
    hp = lax.Precision.HIGHEST
    h = jnp.sin(f_freq * (jnp.dot(z, f_w1, precision=hp) + f_b1))
    h = jnp.sin(f_freq * (jnp.dot(h, f_w2, precision=hp) + f_b2))
    h = jnp.dot(h, f_w3, precision=hp)
    max_decay = math.log(DECAY_TARGET) / FAST_DECAY_PCT
    min_decay = math.log(DECAY_TARGET) / SLOW_DECAY_PCT
    deltas = jnp.abs(jnp.linspace(min_decay, max_decay, dh, dtype=F32))
    decay = jnp.exp(-t * deltas)
    h_fwd = h[:, :dh] * decay
    h_bwd = h[:, dh:] * decay
    k_full = jnp.concatenate([h_fwd, jnp.zeros((1, dh), F32), h_bwd[1:][::-1]], axis=0)
    return k_full / jnp.sum(jnp.abs(k_full), axis=0, keepdims=True)


def _dispatch_tables(eid, n_experts, bm):
    T = eid.shape[0]
    A = T * TOP_K
    flat_e = eid.reshape(A)
    se, order = lax.sort((flat_e, jnp.arange(A, dtype=jnp.int32)), num_keys=1, is_stable=True)
    experts = jnp.arange(n_experts, dtype=jnp.int32)
    starts = jnp.searchsorted(se, experts, side="left").astype(jnp.int32)
    counts = jnp.searchsorted(se, experts, side="right").astype(jnp.int32) - starts
    pcounts = (counts + bm - 1) // bm * bm
    pends = jnp.cumsum(pcounts)
    pstarts = pends - pcounts
    n_blocks = -(-A // bm) + n_experts
    P = n_blocks * bm
    block_e = jnp.minimum(
        jnp.searchsorted(pends, jnp.arange(n_blocks, dtype=jnp.int32) * bm, side="right"),
        n_experts - 1).astype(jnp.int32)
    shift = jnp.sum(jnp.where(se[:, None] == experts[None, :], (pstarts - starts)[None, :], 0), axis=1)
    dest = jnp.arange(A, dtype=jnp.int32) + shift
    row_asg = jnp.full((P,), -1, jnp.int32).at[dest].set(order, unique_indices=True)
    pad = row_asg < 0
    src = jnp.maximum(row_asg, 0) // TOP_K
    dst = jnp.where(pad, A + jnp.cumsum(pad.astype(jnp.int32)) - 1, (row_asg % TOP_K) * T + row_asg // TOP_K)
    n_used = (pends[-1] // bm).astype(jnp.int32).reshape(1)
    return block_e, n_used, src.reshape(n_blocks, 1, bm), dst.astype(jnp.int32).reshape(n_blocks, 1, bm), P


def _pick(n, pref):
    t = min(n, pref)
    assert n % t == 0, (n, t)
    return t


def _trunk(xa, xb, p):
    Ba, L, D = xa.shape
    Bb = xb.shape[0]
    assert xb.shape[1:] == (L, D)
    B = Ba + Bb
    T = B * L
    depth = p["w_in"].shape[0]
    dh = p["h_bias"].shape[1]
    H = p["rel_table"].shape[1]
    hd = p["lam_q1"].shape[1]
    da = H * 2 * hd
    n_groups = p["w_route_group"].shape[2]
    n_experts = p["w_route_expert"].shape[2]
    per_group = n_experts // n_groups
    alpha = (2.0 * depth) ** 0.25
    assert L % FFT_N2 == 0 and (L // FFT_N2) % 8 == 0
    assert 2 * hd == LANES and dh % LANES == 0 and n_groups + n_experts <= LANES

    tm = _pick(L, 512)
    t_attn = _pick(L, 512)
    bm = 512

    xs = _ln_embed(xa.reshape(Ba * L, D), xb.reshape(Bb * L, D), p["ln_emb_g"], p["ln_emb_b"], tm)
    tables = _dft_tables(L)
    nb = L // FFT_N2
    bias = _bias_tiles(p["rel_table"], t_attn)

    for l in range(depth):
        lam_init = 0.8 - 0.6 * math.exp(-0.3 * l)
        w_in = p["w_in"][l]
        wh = w_in[:, :3 * dh].astype(BF16)
        wa = w_in[:, 3 * dh:]
        wqt = (wa[:, :da] * (hd ** -0.5 * LOG2E)).T.astype(BF16)
        wk = wa[:, da:2 * da].astype(BF16)
        wvt = wa[:, 2 * da:].T.astype(BF16)
        uh, k, qt, vt = _proj_in(xs, wh, wk, wqt, wvt, B, L, H, tm)

        vx, x0c = _hyena_pre(uh, p["conv_w"][l], p["conv_b"][l], B, L, tm)
        k_full = _hyena_filter(L, dh, p["f_w1"][l], p["f_b1"][l], p["f_freq"][l], p["f_w2"][l],
                               p["f_b2"][l], p["f_w3"][l])
        h_spec = _filter_spec(k_full.T.reshape(dh, 2 * nb, FFT_N2), tables, LANES)
        vx4 = vx.reshape(B, L, dh).transpose(0, 2, 1).reshape(B, dh, nb, FFT_N2)
        cy = _fft_conv(vx4, tables, h_spec, LANES)
        cy = cy.reshape(B, dh, L).transpose(0, 2, 1).reshape(T, dh)

        lam = (jnp.exp(jnp.sum(p["lam_q1"][l] * p["lam_k1"][l]))
               - jnp.exp(jnp.sum(p["lam_q2"][l] * p["lam_k2"][l])) + lam_init).reshape(1).astype(F32)
        ya = _attention(lam, qt, k, vt, bias, p["subln_g"][l].reshape(2 * hd, 1), B, L, H, hd, t_attn,
                        lam_init)

        w_out = p["w_out"][l]
        w_r = jnp.concatenate([p["w_route_group"][l], p["w_route_expert"][l]], axis=1)
        w_r = jnp.pad(w_r, ((0, 0), (0, LANES - w_r.shape[1])))
        w_r_hi = w_r.astype(BF16)
        w_r_lo = (w_r - w_r_hi.astype(F32)).astype(BF16)
        x1, eid, gates = _proj_out(
            cy, vx, x0c, p["h_bias"][l].reshape(1, dh), ya, xs,
            w_out[:dh].astype(BF16), w_out[dh:].astype(BF16),
            p["ln1_g"][l].reshape(1, D), p["ln1_b"][l].reshape(1, D),
            jnp.concatenate([w_r_hi, w_r_lo], axis=1), w_r_hi,
            alpha, n_groups, per_group, tm)

        block_e, n_used, src, dst, n_rows = _dispatch_tables(eid[:, :TOP_K], n_experts, bm)
        y2 = _moe_experts(block_e, n_used, src, dst, x1, p["w_gate"][l], p["w_up"][l], p["w_down"][l],
                          n_rows, bm)
        xs = _combine_ln(y2, gates, x1, p["ln2_g"][l].reshape(1, D), p["ln2_b"][l].reshape(1, D),
                         alpha, tm, split_rows=Ba * L if l == depth - 1 else None)
    return xs[0].reshape(Ba, L, D), xs[1].reshape(Bb, L, D)


def kernel(x_prompt, x_sample, ln_emb_g, ln_emb_b, rel_table, w_in, conv_w, conv_b, f_w1, f_b1, f_freq,
           f_w2, f_b2, f_w3, h_bias, lam_q1, lam_k1, lam_q2, lam_k2, subln_g, w_out, ln1_g, ln1_b,
           w_route_group, w_route_expert, w_gate, w_up, w_down, ln2_g, ln2_b):
    p = dict(ln_emb_g=ln_emb_g, ln_emb_b=ln_emb_b, rel_table=rel_table, w_in=w_in, conv_w=conv_w,
             conv_b=conv_b, f_w1=f_w1, f_b1=f_b1, f_freq=f_freq, f_w2=f_w2, f_b2=f_b2, f_w3=f_w3,
             h_bias=h_bias, lam_q1=lam_q1, lam_k1=lam_k1, lam_q2=lam_q2, lam_k2=lam_k2,
             subln_g=subln_g, w_out=w_out, ln1_g=ln1_g, ln1_b=ln1_b, w_route_group=w_route_group,
             w_route_expert=w_route_expert, w_gate=w_gate, w_up=w_up, w_down=w_down, ln2_g=ln2_g,
             ln2_b=ln2_b)
    return _trunk(x_prompt, x_sample, p)
```

```python
import functools
import math

import jax
import jax.numpy as jnp
from jax import lax
from jax.experimental import pallas as pl
from jax.experimental.pallas import tpu as pltpu

F32 = jnp.float32
BF16 = jnp.bfloat16

LN_EPS = 1e-5
REL_MAX_DIST = 128
TOP_K = 2
DECAY_TARGET = 1e-2
FAST_DECAY_PCT = 0.3
SLOW_DECAY_PCT = 1.5

LANES = 128
BF16_SUBLANES = 16
LOG2E = 1.4426950408889634
FFT_N2 = 128
VMEM_LIMIT_BYTES = 56 * 1024 * 1024


def _cparams(sem):
    return pltpu.CompilerParams(dimension_semantics=sem, vmem_limit_bytes=VMEM_LIMIT_BYTES)


def _dot(a, b):
    return jnp.dot(a, b, preferred_element_type=F32)


def _layer_norm_rows(z, g, b):
    mu = jnp.mean(z, axis=-1, keepdims=True)
    zc = z - mu
    var = jnp.mean(zc * zc, axis=-1, keepdims=True)
    return zc * lax.rsqrt(var + LN_EPS) * g + b


def _ln_kernel(xa_ref, xb_ref, g_ref, b_ref, o_ref, *, na):
    i = pl.program_id(0)

    @pl.when(i < na)
    def _():
        o_ref[...] = _layer_norm_rows(xa_ref[...], g_ref[...], b_ref[...])

    @pl.when(i >= na)
    def _():
        o_ref[...] = _layer_norm_rows(xb_ref[...], g_ref[...], b_ref[...])


def _ln_embed(xa, xb, g, b, tm):
    Ta, D = xa.shape
    Tb = xb.shape[0]
    na = Ta // tm
    nb = Tb // tm
    return pl.pallas_call(
        functools.partial(_ln_kernel, na=na),
        grid=(na + nb,),
        in_specs=[
            pl.BlockSpec((tm, D), lambda i: (jnp.minimum(i, na - 1), 0)),
            pl.BlockSpec((tm, D), lambda i: (jnp.maximum(i - na, 0), 0)),
            pl.BlockSpec((1, D), lambda i: (0, 0)),
            pl.BlockSpec((1, D), lambda i: (0, 0)),
        ],
        out_specs=pl.BlockSpec((tm, D), lambda i: (i, 0)),
        out_shape=jax.ShapeDtypeStruct((Ta + Tb, D), F32),
        compiler_params=_cparams(("arbitrary",)),
        name="ln_embed",
    )(xa, xb, g.reshape(1, D), b.reshape(1, D))


def _proj_in_kernel(x_ref, wh_ref, wk_ref, wqt_ref, wvt_ref, uh_ref, k_ref, qt_ref, vt_ref):
    xb = x_ref[...].astype(BF16)
    uh_ref[...] = _dot(xb, wh_ref[...]).astype(BF16)
    k_ref[...] = _dot(xb, wk_ref[...]).astype(BF16)
    nt = (((1,), (1,)), ((), ()))
    qt_ref[...] = lax.dot_general(wqt_ref[...], xb, nt, preferred_element_type=F32).astype(BF16)
    vt = lax.dot_general(wvt_ref[...], xb, nt, preferred_element_type=F32).astype(BF16)
    n_heads, dvp, tm = vt_ref.shape
    dv = vt.shape[0] // n_heads
    pad_rows = lax.broadcasted_iota(jnp.int32, (dvp - dv, tm), 0)
    ones_pad = jnp.where(pad_rows == 0, 1.0, 0.0).astype(BF16)
    for h in range(n_heads):
        vt_ref[h, 0:dv, :] = vt[h * dv:(h + 1) * dv]
        vt_ref[h, dv:dvp, :] = ones_pad


def _proj_in(x2d, wh, wk, wqt, wvt, B, L, H, tm):
    T, D = x2d.shape
    Ch = wh.shape[1]
    Da = wk.shape[1]
    nl = L // tm
    dvp = Da // H + BF16_SUBLANES
    return pl.pallas_call(
        _proj_in_kernel,
        grid=(B, nl),
        in_specs=[
            pl.BlockSpec((tm, D), lambda b, i: (b * nl + i, 0)),
            pl.BlockSpec((D, Ch), lambda b, i: (0, 0)),
            pl.BlockSpec((D, Da), lambda b, i: (0, 0)),
            pl.BlockSpec((Da, D), lambda b, i: (0, 0)),
            pl.BlockSpec((Da, D), lambda b, i: (0, 0)),
        ],
        out_specs=[
            pl.BlockSpec((tm, Ch), lambda b, i: (b * nl + i, 0)),
            pl.BlockSpec((tm, Da), lambda b, i: (b * nl + i, 0)),
            pl.BlockSpec((None, Da, tm), lambda b, i: (b, 0, i)),
            pl.BlockSpec((None, H, dvp, tm), lambda b, i: (b, 0, 0, i)),
        ],
        out_shape=[
            jax.ShapeDtypeStruct((T, Ch), BF16),
            jax.ShapeDtypeStruct((T, Da), BF16),
            jax.ShapeDtypeStruct((B, Da, L), BF16),
            jax.ShapeDtypeStruct((B, H, dvp, L), BF16),
        ],
        compiler_params=_cparams(("arbitrary", "arbitrary")),
        name="proj_in",
    )(x2d, wh, wk, wqt, wvt)


def _hyena_pre_kernel(u_ref, up_ref, un_ref, w_ref, b_ref, vx_ref, x0_ref, *, dh, halo):
    i = pl.program_id(1)
    last = pl.num_programs(1) - 1
    u = u_ref[...].astype(F32)
    tm = u.shape[0]
    prev_row = up_ref[halo - 1:halo, :].astype(F32)
    next_row = un_ref[0:1, :].astype(F32)
    prev_row = jnp.where(i == 0, 0.0, prev_row)
    next_row = jnp.where(i == last, 0.0, next_row)
    rows = lax.broadcasted_iota(jnp.int32, (tm, 1), 0)
    um1 = jnp.where(rows == 0, prev_row, pltpu.roll(u, 1, axis=0))
    up1 = jnp.where(rows == tm - 1, next_row, pltpu.roll(u, tm - 1, axis=0))
    w = w_ref[...]
    y = b_ref[...] + um1 * w[0:1, :]
    y = y + u * w[1:2, :]
    y = y + up1 * w[2:3, :]
    x0_ref[...] = y[:, :dh].astype(BF16)
    vx_ref[...] = y[:, 2 * dh:] * y[:, dh:2 * dh]


def _hyena_pre(uh, conv_w, conv_b, B, L, tm):
    T, Ch = uh.shape
    dh = Ch // 3
    halo = 16
    nl = L // tm
    r = tm // halo
    nhalo = T // halo
    return pl.pallas_call(
        functools.partial(_hyena_pre_kernel, dh=dh, halo=halo),
        grid=(B, nl),
        in_specs=[
            pl.BlockSpec((tm, Ch), lambda b, i: (b * nl + i, 0)),
            pl.BlockSpec((halo, Ch), lambda b, i: (jnp.maximum((b * nl + i) * r - 1, 0), 0)),
            pl.BlockSpec((halo, Ch), lambda b, i: (jnp.minimum((b * nl + i + 1) * r, nhalo - 1), 0)),
            pl.BlockSpec((3, Ch), lambda b, i: (0, 0)),
            pl.BlockSpec((1, Ch), lambda b, i: (0, 0)),
        ],
        out_specs=[
            pl.BlockSpec((tm, dh), lambda b, i: (b * nl + i, 0)),
            pl.BlockSpec((tm, dh), lambda b, i: (b * nl + i, 0)),
        ],
        out_shape=[
            jax.ShapeDtypeStruct((T, dh), F32),
            jax.ShapeDtypeStruct((T, dh), BF16),
        ],
        compiler_params=_cparams(("arbitrary", "arbitrary")),
        name="hyena_pre",
    )(uh, uh, uh, conv_w, conv_b.reshape(1, Ch))


def _dft_tables(L):
    N = 2 * L
    N1 = N // FFT_N2
    NB = L // FFT_N2
    r1 = jnp.arange(N1, dtype=jnp.int32)
    r2 = jnp.arange(FFT_N2, dtype=jnp.int32)

    def cos_sin(prod, period):
        ang = (prod % period).astype(F32) * (2.0 * math.pi / period)
        return jnp.cos(ang), jnp.sin(ang)

    c1, s1 = cos_sin(r1[:, None] * r1[None, :], N1)
    f1 = jnp.concatenate([c1, -s1], axis=0).astype(BF16)
    f1i = jnp.concatenate([c1[:NB], -s1[:NB]], axis=1).astype(BF16)
    ct, st = cos_sin(r1[:, None] * r2[None, :], N)
    c2, s2 = cos_sin(r2[:, None] * r2[None, :], FFT_N2)
    f2 = jnp.concatenate([c2, -s2], axis=1).astype(BF16)
    f2i = jnp.concatenate([jnp.concatenate([c2, s2], 1),
                           jnp.concatenate([-s2, c2], 1)], 0).astype(BF16)
    return dict(f1=f1, f1d=f1[:, :NB], f1i=f1i, tr=ct, ti=-st, f2=f2, f2i=f2i)


FFT_GROUP = 8


def _dft2_forward(xs, f1_ref, tr, ti, f2_ref):
    n1 = tr.shape[0]
    a = _dot(f1_ref[...], jnp.concatenate(xs, axis=1))
    rows = []
    for j in range(len(xs)):
        aj = a[:, j * FFT_N2:(j + 1) * FFT_N2]
        ar, ai = aj[:n1], aj[n1:]
        rows += [ar * tr - ai * ti, ar * ti + ai * tr]
    p = _dot(jnp.concatenate(rows, axis=0).astype(BF16), f2_ref[...])
    out = []
    for j in range(len(xs)):
        pr, pi = p[2 * j * n1:(2 * j + 1) * n1], p[(2 * j + 1) * n1:(2 * j + 2) * n1]
        out.append((pr[:, :FFT_N2] - pi[:, FFT_N2:], pr[:, FFT_N2:] + pi[:, :FFT_N2]))
    return out


def _filter_spec_kernel(kf_ref, f1_ref, tr_ref, ti_ref, f2_ref, h_ref, *, inv_n, unroll):
    tr, ti = tr_ref[...], ti_ref[...]

    def group(g, carry):
        c0 = g * FFT_GROUP
        xs = [kf_ref[c0 + j].astype(BF16) for j in range(FFT_GROUP)]
        for j, (sr, si) in enumerate(_dft2_forward(xs, f1_ref, tr, ti, f2_ref)):
            h_ref[c0 + j] = (jnp.concatenate([sr, si], axis=1) * inv_n).astype(BF16)
        return carry

    lax.fori_loop(0, kf_ref.shape[0] // FFT_GROUP, group, 0, unroll=unroll)


def _filter_spec(kf, tb, cc):
    Dh, n1, _ = kf.shape
    fix2 = lambda c: (0, 0)
    return pl.pallas_call(
        functools.partial(_filter_spec_kernel, inv_n=1.0 / (n1 * FFT_N2), unroll=2),
        grid=(Dh // cc,),
        in_specs=[
            pl.BlockSpec((cc, n1, FFT_N2), lambda c: (c, 0, 0)),
            pl.BlockSpec((2 * n1, n1), fix2),
            pl.BlockSpec((n1, FFT_N2), fix2),
            pl.BlockSpec((n1, FFT_N2), fix2),
            pl.BlockSpec((FFT_N2, 2 * FFT_N2), fix2),
        ],
        out_specs=pl.BlockSpec((cc, n1, 2 * FFT_N2), lambda c: (c, 0, 0)),
        out_shape=jax.ShapeDtypeStruct((Dh, n1, 2 * FFT_N2), BF16),
        compiler_params=_cparams(("arbitrary",)),
        name="filter_spec",
    )(kf, tb["f1"], tb["tr"], tb["ti"], tb["f2"])


def _fft_conv_kernel(x_ref, f1_ref, f1i_ref, tr_ref, ti_ref, f2_ref, f2i_ref, h_ref, o_ref, *, unroll):
    tr, ti = tr_ref[...], ti_ref[...]
    n1 = tr.shape[0]

    def group(g, carry):
        c0 = g * FFT_GROUP
        xs = [x_ref[c0 + j].astype(BF16) for j in range(FFT_GROUP)]
        ys = []
        for j, (sr, si) in enumerate(_dft2_forward(xs, f1_ref, tr, ti, f2_ref)):
            h = h_ref[c0 + j].astype(F32)
            hr, hi = h[:, :FFT_N2], h[:, FFT_N2:]
            ys.append(jnp.concatenate([sr * hr - si * hi, sr * hi + si * hr], axis=1))
        b = _dot(jnp.concatenate(ys, axis=0).astype(BF16), f2i_ref[...])
        cols = []
        for j in range(FFT_GROUP):
            bj = b[j * n1:(j + 1) * n1]
            br, bi = bj[:, :FFT_N2], bj[:, FFT_N2:]
            cols.append(jnp.concatenate([br * tr + bi * ti, bi * tr - br * ti], axis=0))
        y = _dot(f1i_ref[...], jnp.concatenate(cols, axis=1).astype(BF16))
        for j in range(FFT_GROUP):
            o_ref[c0 + j] = y[:, j * FFT_N2:(j + 1) * FFT_N2]
        return carry

    lax.fori_loop(0, x_ref.shape[0] // FFT_GROUP, group, 0, unroll=unroll)


def _fft_conv(x4, tb, h_spec, cc):
    B, Dh, nb, _ = x4.shape
    n1 = 2 * nb
    fix2 = lambda c, b: (0, 0)
    return pl.pallas_call(
        functools.partial(_fft_conv_kernel, unroll=2),
        grid=(Dh // cc, B),
        in_specs=[
            pl.BlockSpec((None, cc, nb, FFT_N2), lambda c, b: (b, c, 0, 0)),
            pl.BlockSpec((2 * n1, nb), fix2),
            pl.BlockSpec((nb, 2 * n1), fix2),
            pl.BlockSpec((n1, FFT_N2), fix2),
            pl.BlockSpec((n1, FFT_N2), fix2),
            pl.BlockSpec((FFT_N2, 2 * FFT_N2), fix2),
            pl.BlockSpec((2 * FFT_N2, 2 * FFT_N2), fix2),
            pl.BlockSpec((cc, n1, 2 * FFT_N2), lambda c, b: (c, 0, 0)),
        ],
        out_specs=pl.BlockSpec((None, cc, nb, FFT_N2), lambda c, b: (b, c, 0, 0)),
        out_shape=jax.ShapeDtypeStruct((B, Dh, nb, FFT_N2), F32),
        compiler_params=_cparams(("arbitrary", "arbitrary")),
        name="fft_conv",
    )(x4, tb["f1d"], tb["f1i"], tb["tr"], tb["ti"], tb["f2"], tb["f2i"], h_spec)


def _bias_tiles_kernel(table_ref, o_ref, *, t, n_buckets):
    h = pl.program_id(0)
    c = pl.program_id(1)
    nb = n_buckets // 2
    max_exact = nb // 2
    kv = lax.broadcasted_iota(jnp.int32, (t, t), 0)
    q = lax.broadcasted_iota(jnp.int32, (t, t), 1)
    rel = (c - 2) * t + kv - q
    ret = jnp.where(rel > 0, nb, 0)
    n = jnp.abs(rel)
    large = max_exact + (jnp.log(jnp.maximum(n, 1).astype(F32) / max_exact)
                         / math.log(REL_MAX_DIST / max_exact) * (nb - max_exact)).astype(jnp.int32)
    large = jnp.minimum(large, nb - 1)
    bucket = ret + jnp.where(n < max_exact, n, large)
    out = jnp.zeros((t, t), F32)
    for b in range(n_buckets):
        out = jnp.where(bucket == b, table_ref[b, h], out)
    o_ref[...] = out * LOG2E


def _bias_tiles(rel_table, t):
    n_buckets, H = rel_table.shape
    return pl.pallas_call(
        functools.partial(_bias_tiles_kernel, t=t, n_buckets=n_buckets),
        grid=(H, 5),
        in_specs=[pl.BlockSpec(memory_space=pltpu.SMEM)],
        out_specs=pl.BlockSpec((None, None, t, t), lambda h, c: (h, c, 0, 0)),
        out_shape=jax.ShapeDtypeStruct((H, 5, t, t), F32),
        compiler_params=_cparams(("arbitrary", "arbitrary")),
        name="bias_tiles",
    )(rel_table)


def _attn_kernel(lam_ref, qt_ref, k_ref, vt_ref, bias_ref, g_ref, o_ref, acc1_ref, acc2_ref,
                 sa_ref, sb_ref, *, t, nk, hd, out_scale):
    qi = pl.program_id(2)
    dv = 2 * hd
    qt = qt_ref[...]
    half = lax.broadcasted_iota(jnp.int32, qt.shape, 0) < hd
    zero = jnp.zeros_like(qt)
    q1 = jnp.where(half, qt, zero)
    q2 = jnp.where(half, zero, qt)
    acc1_ref[...] = jnp.zeros_like(acc1_ref)
    acc2_ref[...] = jnp.zeros_like(acc2_ref)

    def scores(ki, s_ref):
        off = pl.multiple_of(ki * t, t)
        kb = k_ref[pl.ds(off, t), :]
        bias = bias_ref[jnp.clip(ki - qi, -2, 2) + 2]
        s1 = _dot(kb, q1) + bias
        s_ref[0] = s1
        s2 = _dot(kb, q2) + bias
        s_ref[1] = s2
        return jnp.max(s1, axis=0, keepdims=True), jnp.max(s2, axis=0, keepdims=True)

    def accumulate(ki, s_ref, mt, m):
        off = pl.multiple_of(ki * t, t)
        vb = vt_ref[:, pl.ds(off, t)]
        out = []
        for j, acc_ref in enumerate((acc1_ref, acc2_ref)):
            m_new = jnp.maximum(m[j], mt[j])
            alpha = jnp.exp2(m[j] - m_new)
            p = jnp.exp2((s_ref[j] - m_new).astype(BF16))
            acc_ref[...] = alpha * acc_ref[...] + _dot(vb, p)
            out.append(m_new)
        return tuple(out)

    def pair(j, carry):
        mt_a, m = carry
        ka = 2 * j
        mt_b = scores(ka + 1, sb_ref)
        m = accumulate(ka, sa_ref, mt_a, m)
        mt_a = scores(jnp.minimum(ka + 2, nk - 1), sa_ref)
        m = accumulate(ka + 1, sb_ref, mt_b, m)
        return mt_a, m

    neg = jnp.full((1, t), -jnp.inf, F32)
    mt0 = scores(0, sa_ref)
    _, (m1, m2) = lax.fori_loop(0, nk // 2, pair, (mt0, (neg, neg)), unroll=2)
    a1 = acc1_ref[...]
    a2 = acc2_ref[...]
    o = a1[:dv] / a1[dv:dv + 1] - lam_ref[0] * (a2[:dv] / a2[dv:dv + 1])
    ms = jnp.mean(o * o, axis=0, keepdims=True)
    o = o * lax.rsqrt(ms + LN_EPS) * g_ref[...] * out_scale
    o_ref[...] = o.T.astype(BF16)


def _attention(lam, qt, k, vt, bias, g_col, B, L, H, hd, t, lam_init):
    Da = H * 2 * hd
    nq = L // t
    assert nq % 2 == 0
    dvp = vt.shape[2]
    k3 = k.reshape(B, L, Da)
    out = pl.pallas_call(
        functools.partial(_attn_kernel, t=t, nk=nq, hd=hd, out_scale=1.0 - lam_init),
        grid=(B, H, nq),
        in_specs=[
            pl.BlockSpec(memory_space=pltpu.SMEM),
            pl.BlockSpec((None, 2 * hd, t), lambda b, h, q: (b, h, q)),
            pl.BlockSpec((None, L, 2 * hd), lambda b, h, q: (b, 0, h)),
            pl.BlockSpec((None, None, dvp, L), lambda b, h, q: (b, h, 0, 0)),
            pl.BlockSpec((None, 5, t, t), lambda b, h, q: (h, 0, 0, 0)),
            pl.BlockSpec((2 * hd, 1), lambda b, h, q: (0, 0)),
        ],
        out_specs=pl.BlockSpec((None, t, 2 * hd), lambda b, h, q: (b, q, h)),
        out_shape=jax.ShapeDtypeStruct((B, L, Da), BF16),
        scratch_shapes=[pltpu.VMEM((dvp, t), F32), pltpu.VMEM((dvp, t), F32),
                        pltpu.VMEM((2, t, t), F32), pltpu.VMEM((2, t, t), F32)],
        compiler_params=_cparams(("arbitrary", "arbitrary", "arbitrary")),
        name="diff_attn",
    )(lam, qt, k3, vt, bias, g_col)
    return out.reshape(B * L, Da)


def _proj_out_kernel(cy_ref, vx_ref, x0_ref, hb_ref, ya_ref, x_ref, woh_ref, woa_ref, g_ref, b_ref,
                     wrc_ref, wrh_ref, x1_ref, eid_ref, gate_ref, *, alpha, n_groups, per_group):
    yh = ((cy_ref[...] + vx_ref[...] * hb_ref[...]) * x0_ref[...].astype(F32)).astype(BF16)
    mix = _dot(yh, woh_ref[...]) + _dot(ya_ref[...], woa_ref[...])
    x1 = _layer_norm_rows(alpha * x_ref[...] + mix, g_ref[...], b_ref[...])
    x1_ref[...] = x1
    hi = x1.astype(BF16)
    lo = (x1 - hi.astype(F32)).astype(BF16)
    lg2 = _dot(hi, wrc_ref[...])
    lg = lg2[:, :LANES] + lg2[:, LANES:] + _dot(lo, wrh_ref[...])
    col = lax.broadcasted_iota(jnp.int32, lg.shape, 1).astype(F32)
    far = jnp.float32(LANES)
    neg_inf = jnp.float32(-jnp.inf)

    def first_col(mask):
        return jnp.min(jnp.where(mask, col, far), axis=-1, keepdims=True)

    gmask = col < n_groups
    glog = jnp.where(gmask, lg, neg_inf)
    gmax = jnp.max(glog, axis=-1, keepdims=True)
    g_sel = first_col(glog == gmax)
    g_w = 1.0 / jnp.sum(jnp.where(gmask, jnp.exp(glog - gmax), 0.0), axis=-1, keepdims=True)
    lo_col = n_groups + g_sel * per_group
    emask = (col >= lo_col) & (col < lo_col + per_group)
    elog = jnp.where(emask, lg, neg_inf)
    emax = jnp.max(elog, axis=-1, keepdims=True)
    eexp = jnp.where(emask, jnp.exp(elog - emax), 0.0)
    prob = eexp / jnp.sum(eexp, axis=-1, keepdims=True)
    p1 = jnp.max(prob, axis=-1, keepdims=True)
    i1 = first_col(emask & (prob == p1))
    mask2 = emask & (col != i1)
    p2 = jnp.max(jnp.where(mask2, prob, -1.0), axis=-1, keepdims=True)
    i2 = first_col(mask2 & (prob == p2))
    psum = p1 + p2
    eid = jnp.where(col == 0, i1 - n_groups, jnp.where(col == 1, i2 - n_groups, 0.0))
    eid_ref[...] = eid.astype(jnp.int32)
    gate_ref[...] = jnp.where(col == 0, g_w * p1 / psum, jnp.where(col == 1, g_w * p2 / psum, 0.0))


def _proj_out(cy, vx, x0c, hb, ya, x, woh, woa, g, b, wrc, wrh, alpha, n_groups, per_group, tm):
    T, D = x.shape
    Dh = cy.shape[1]
    Da = ya.shape[1]
    row = lambda i: (i, 0)
    fix = lambda i: (0, 0)
    return pl.pallas_call(
        functools.partial(_proj_out_kernel, alpha=alpha, n_groups=n_groups, per_group=per_group),
        grid=(T // tm,),
        in_specs=[
            pl.BlockSpec((tm, Dh), row),
            pl.BlockSpec((tm, Dh), row),
            pl.BlockSpec((tm, Dh), row),
            pl.BlockSpec((1, Dh), fix),
            pl.BlockSpec((tm, Da), row),
            pl.BlockSpec((tm, D), row),
            pl.BlockSpec((Dh, D), fix),
            pl.BlockSpec((Da, D), fix),
            pl.BlockSpec((1, D), fix),
            pl.BlockSpec((1, D), fix),
            pl.BlockSpec((D, 2 * LANES), fix),
            pl.BlockSpec((D, LANES), fix),
        ],
        out_specs=[
            pl.BlockSpec((tm, D), row),
            pl.BlockSpec((tm, LANES), row),
            pl.BlockSpec((tm, LANES), row),
        ],
        out_shape=[
            jax.ShapeDtypeStruct((T, D), F32),
            jax.ShapeDtypeStruct((T, LANES), jnp.int32),
            jax.ShapeDtypeStruct((T, LANES), F32),
        ],
        compiler_params=_cparams(("arbitrary",)),
        name="proj_out_ln_router",
    )(cy, vx, x0c, hb, ya, x, woh, woa, g, b, wrc, wrh)


def _moe_kernel(be_ref, nu_ref, src_ref, srcn_ref, dst_ref, x_hbm, wg_ref, wu_ref, wd_ref, out_hbm,
                xbuf, ybuf, wgb, wub, wdb, sem_in, sem_out, *, bm, chunk):
    i = pl.program_id(0)
    n_used = nu_ref[0]
    slot = i % 2

    def gather_row(idx_ref, s, r):
        return pltpu.make_async_copy(x_hbm.at[pl.ds(idx_ref[0, r], 1), :],
                                     xbuf.at[s, pl.ds(r, 1), :], sem_in.at[s])

    def scatter_row(s, r):
        return pltpu.make_async_copy(ybuf.at[s, pl.ds(r, 1), :],
                                     out_hbm.at[pl.ds(dst_ref[0, r], 1), :], sem_out.at[s])

    def wait_gather(s):
        pltpu.make_async_copy(x_hbm.at[pl.ds(0, bm), :], xbuf.at[s], sem_in.at[s]).wait()

    def wait_scatter(s):
        pltpu.make_async_copy(ybuf.at[s], out_hbm.at[pl.ds(0, bm), :], sem_out.at[s]).wait()

    @pl.when(i == 0)
    def _():
        def body(r, c):
            gather_row(src_ref, 0, r).start()
            return c

        lax.fori_loop(0, bm, body, 0, unroll=8)

    def run(cur, nxt):
        changed = jnp.logical_or(i == 0, be_ref[i] != be_ref[jnp.maximum(i - 1, 0)])

        @pl.when(changed)
        def _():
            wgb[...] = wg_ref[...].astype(BF16)
            wub[...] = wu_ref[...].astype(BF16)
            wdb[...] = wd_ref[...].astype(BF16)

        wait_gather(cur)
        for r in range(bm):
            gather_row(srcn_ref, nxt, r).start()
        xb = xbuf[cur].astype(BF16)
        hg = _dot(xb, wgb[...])
        hu = _dot(xb, wub[...])
        act = ((hg / (1.0 + jnp.exp(-hg))) * hu).astype(BF16)
        for c in range(bm // chunk):
            rows = slice(c * chunk, (c + 1) * chunk)
            ybuf[cur, rows, :] = _dot(act[rows], wdb[...])
            for r in range(c * chunk, (c + 1) * chunk):
                scatter_row(cur, r).start()

        @pl.when(i > 0)
        def _():
            wait_scatter(nxt)

        @pl.when(i == n_used - 1)
        def _():
            wait_scatter(cur)
            wait_gather(nxt)

    for parity in (0, 1):
        @pl.when(jnp.logical_and(i < n_used, slot == parity))
        def _():
            run(parity, 1 - parity)


def _moe_experts(block_e, n_used, src, dst, x1, w_gate, w_up, w_down, n_rows_out, bm):
    n_blocks = block_e.shape[0]
    T, D = x1.shape
    E, _, De = w_gate.shape
    idx_block = (None, 1, bm)
    grid_spec = pltpu.PrefetchScalarGridSpec(
        num_scalar_prefetch=2,
        grid=(n_blocks,),
        in_specs=[
            pl.BlockSpec(idx_block, lambda i, be, nu: (i, 0, 0), memory_space=pltpu.SMEM),
            pl.BlockSpec(idx_block, lambda i, be, nu: (jnp.minimum(i + 1, n_blocks - 1), 0, 0),
                         memory_space=pltpu.SMEM),
            pl.BlockSpec(idx_block, lambda i, be, nu: (i, 0, 0), memory_space=pltpu.SMEM),
            pl.BlockSpec(memory_space=pl.ANY),
            pl.BlockSpec((None, D, De), lambda i, be, nu: (be[i], 0, 0)),
            pl.BlockSpec((None, D, De), lambda i, be, nu: (be[i], 0, 0)),
            pl.BlockSpec((None, De, D), lambda i, be, nu: (be[i], 0, 0)),
        ],
        out_specs=pl.BlockSpec(memory_space=pl.ANY),
        scratch_shapes=[
            pltpu.VMEM((2, bm, D), F32),
            pltpu.VMEM((2, bm, D), F32),
            pltpu.VMEM((D, De), BF16),
            pltpu.VMEM((D, De), BF16),
            pltpu.VMEM((De, D), BF16),
            pltpu.SemaphoreType.DMA((2,)),
            pltpu.SemaphoreType.DMA((2,)),
        ],
    )
    return pl.pallas_call(
        functools.partial(_moe_kernel, bm=bm, chunk=min(bm, 128)),
        grid_spec=grid_spec,
        out_shape=jax.ShapeDtypeStruct((n_rows_out, D), F32),
        compiler_params=_cparams(("arbitrary",)),
        name="moe_experts",
    )(block_e, n_used, src, src, dst, x1, w_gate, w_up, w_down)


def _combine_kernel(ya_ref, yb_ref, gate_ref, x1_ref, g_ref, b_ref, *o_refs, alpha, na):
    gates = gate_ref[...]
    y = gates[:, 0:1] * ya_ref[...] + gates[:, 1:2] * yb_ref[...]
    res = _layer_norm_rows(alpha * x1_ref[...] + y, g_ref[...], b_ref[...])
    if len(o_refs) == 1:
        o_refs[0][...] = res
    else:
        i = pl.program_id(0)

        @pl.when(i < na)
        def _():
            o_refs[0][...] = res

        @pl.when(i >= na)
        def _():
            o_refs[1][...] = res


def _combine_ln(y2, gates, x1, g, b, alpha, tm, split_rows=None):
    T, D = x1.shape
    nt = T // tm
    row = lambda i: (i, 0)
    fix = lambda i: (0, 0)
    if split_rows is None:
        na = nt
        out_specs = pl.BlockSpec((tm, D), row)
        out_shape = jax.ShapeDtypeStruct((T, D), F32)
    else:
        na = split_rows // tm
        out_specs = [pl.BlockSpec((tm, D), lambda i: (jnp.minimum(i, na - 1), 0)),
                     pl.BlockSpec((tm, D), lambda i: (jnp.maximum(i - na, 0), 0))]
        out_shape = [jax.ShapeDtypeStruct((split_rows, D), F32),
                     jax.ShapeDtypeStruct((T - split_rows, D), F32)]
    return pl.pallas_call(
        functools.partial(_combine_kernel, alpha=alpha, na=na),
        grid=(nt,),
        in_specs=[
            pl.BlockSpec((tm, D), row),
            pl.BlockSpec((tm, D), lambda i: (nt + i, 0)),
            pl.BlockSpec((tm, LANES), row),
            pl.BlockSpec((tm, D), row),
            pl.BlockSpec((1, D), fix),
            pl.BlockSpec((1, D), fix),
        ],
        out_specs=out_specs,
        out_shape=out_shape,
        compiler_params=_cparams(("arbitrary",)),
        name="moe_combine_ln",
    )(y2, y2, gates, x1, g, b)


def _hyena_filter(L, dh, f_w1, f_b1, f_freq, f_w2, f_b2, f_w3):
    emb = f_w1.shape[0]
    bands = (emb - 1) // 2
    t = jnp.linspace(0.0, 1.0, L, dtype=F32)[:, None]
    w = (2.0 * math.pi / L) * jnp.arange(L, dtype=F32)[:, None]
    f = jnp.linspace(1e-4, bands - 1, bands, dtype=F32)[None, :]
    z = jnp.concatenate([t, jnp.cos(f * w), -jnp.sin(f * w)], axis=-1)
    hp = lax.Precision.HIGHEST
    max_decay = math.log(DECAY_TARGET) / FAST_DECAY_PCT
    min_decay = math.log(DECAY_TARGET) / SLOW_DECAY_PCT
    deltas = jnp.abs(jnp.linspace(min_decay, max_decay, dh, dtype=F32))

    def taps(zz, tt, w3):
        h = jnp.sin(f_freq * (jnp.dot(zz, f_w1, precision=hp) + f_b1))
        h = jnp.sin(f_freq * (jnp.dot(h, f_w2, precision=hp) + f_b2))
        return jnp.dot(h, w3, precision=hp) * jnp.exp(-tt * deltas)

    h_fwd = taps(z, t, f_w3[:, :dh])
    h_bwd_rev = taps(z[::-1], t[::-1], f_w3[:, dh:])[:L - 1]
    k_full = jnp.concatenate([h_fwd, jnp.zeros((1, dh), F32), h_bwd_rev], axis=0)
    return k_full / jnp.sum(jnp.abs(k_full), axis=0, keepdims=True)


def _dispatch_tables(eid, n_experts, bm):
    T = eid.shape[0]
    A = T * TOP_K
    flat_e = eid.reshape(A)
    se, order = lax.sort((flat_e, jnp.arange(A, dtype=jnp.int32)), num_keys=1, is_stable=True)
    experts = jnp.arange(n_experts, dtype=jnp.int32)
    onehot = se[:, None] == experts[None, :]
    counts = jnp.sum(onehot, axis=0, dtype=jnp.int32)
    starts = jnp.cumsum(counts) - counts
    pcounts = (counts + bm - 1) // bm * bm
    pends = jnp.cumsum(pcounts)
    pstarts = pends - pcounts
    n_blocks = -(-A // bm) + n_experts
    P = n_blocks * bm
    first_row = jnp.arange(n_blocks, dtype=jnp.int32) * bm
    block_e = jnp.minimum(jnp.sum(pends[None, :] <= first_row[:, None], axis=1, dtype=jnp.int32),
                          n_experts - 1)
    shift = jnp.sum(jnp.where(onehot, (pstarts - starts)[None, :], 0), axis=1)
    dest = jnp.arange(A, dtype=jnp.int32) + shift
    row_asg = jnp.full((P,), -1, jnp.int32).at[dest].set(order, unique_indices=True,
                                                         indices_are_sorted=True)
    pad = row_asg < 0
    src = jnp.maximum(row_asg, 0) // TOP_K
    dst = jnp.where(pad, A + jnp.cumsum(pad.astype(jnp.int32)) - 1, (row_asg % TOP_K) * T + row_asg // TOP_K)
    n_used = (pends[-1] // bm).astype(jnp.int32).reshape(1)
    return block_e, n_used, src.reshape(n_blocks, 1, bm), dst.astype(jnp.int32).reshape(n_blocks, 1, bm), P


def _pick(n, pref):
    t = min(n, pref)
    assert n % t == 0, (n, t)
    return t


def _trunk(xa, xb, p):
    Ba, L, D = xa.shape
    Bb = xb.shape[0]
    assert xb.shape[1:] == (L, D)
    B = Ba + Bb
    T = B * L
    depth = p["w_in"].shape[0]
    dh = p["h_bias"].shape[1]
    H = p["rel_table"].shape[1]
    hd = p["lam_q1"].shape[1]
    da = H * 2 * hd
    n_groups = p["w_route_group"].shape[2]
    n_experts = p["w_route_expert"].shape[2]
    per_group = n_experts // n_groups
    alpha = (2.0 * depth) ** 0.25
    assert L % FFT_N2 == 0 and (L // FFT_N2) % 8 == 0
    assert 2 * hd == LANES and dh % LANES == 0 and n_groups + n_experts <= LANES

    tm = _pick(L, 512)
    t_attn = _pick(L, 512)
    bm = 512

    xs = _ln_embed(xa.reshape(Ba * L, D), xb.reshape(Bb * L, D), p["ln_emb_g"], p["ln_emb_b"], tm)
    tables = _dft_tables(L)
    nb = L // FFT_N2
    bias = _bias_tiles(p["rel_table"], t_attn)

    for l in range(depth):
        lam_init = 0.8 - 0.6 * math.exp(-0.3 * l)
        w_in = p["w_in"][l]
        wh = w_in[:, :3 * dh].astype(BF16)
        wa = w_in[:, 3 * dh:]
        wqt = (wa[:, :da] * (hd ** -0.5 * LOG2E)).T.astype(BF16)
        wk = wa[:, da:2 * da].astype(BF16)
        wvt = wa[:, 2 * da:].T.astype(BF16)
        uh, k, qt, vt = _proj_in(xs, wh, wk, wqt, wvt, B, L, H, tm)

        vx, x0c = _hyena_pre(uh, p["conv_w"][l], p["conv_b"][l], B, L, tm)
        k_full = _hyena_filter(L, dh, p["f_w1"][l], p["f_b1"][l], p["f_freq"][l], p["f_w2"][l],
                               p["f_b2"][l], p["f_w3"][l])
        h_spec = _filter_spec(k_full.T.reshape(dh, 2 * nb, FFT_N2), tables, LANES)
        vx4 = vx.reshape(B, L, dh).transpose(0, 2, 1).reshape(B, dh, nb, FFT_N2)
        cy = _fft_conv(vx4, tables, h_spec, LANES)
        cy = cy.reshape(B, dh, L).transpose(0, 2, 1).reshape(T, dh)

        lam = (jnp.exp(jnp.sum(p["lam_q1"][l] * p["lam_k1"][l]))
               - jnp.exp(jnp.sum(p["lam_q2"][l] * p["lam_k2"][l])) + lam_init).reshape(1).astype(F32)
        ya = _attention(lam, qt, k, vt, bias, p["subln_g"][l].reshape(2 * hd, 1), B, L, H, hd, t_attn,
                        lam_init)

        w_out = p["w_out"][l]
        w_r = jnp.concatenate([p["w_route_group"][l], p["w_route_expert"][l]], axis=1)
        w_r = jnp.pad(w_r, ((0, 0), (0, LANES - w_r.shape[1])))
        w_r_hi = w_r.astype(BF16)
        w_r_lo = (w_r - w_r_hi.astype(F32)).astype(BF16)
        x1, eid, gates = _proj_out(
            cy, vx, x0c, p["h_bias"][l].reshape(1, dh), ya, xs,
            w_out[:dh].astype(BF16), w_out[dh:].astype(BF16),
            p["ln1_g"][l].reshape(1, D), p["ln1_b"][l].reshape(1, D),
            jnp.concatenate([w_r_hi, w_r_lo], axis=1), w_r_hi,
            alpha, n_groups, per_group, tm)

        block_e, n_used, src, dst, n_rows = _dispatch_tables(eid[:, :TOP_K], n_experts, bm)
        y2 = _moe_experts(block_e, n_used, src, dst, x1, p["w_gate"][l], p["w_up"][l], p["w_down"][l],
                          n_rows, bm)
        xs = _combine_ln(y2, gates, x1, p["ln2_g"][l].reshape(1, D), p["ln2_b"][l].reshape(1, D),
                         alpha, tm, split_rows=Ba * L if l == depth - 1 else None)
    return xs[0].reshape(Ba, L, D), xs[1].reshape(Bb, L, D)


def kernel(x_prompt, x_sample, ln_emb_g, ln_emb_b, rel_table, w_in, conv_w, conv_b, f_w1, f_b1, f_freq,
           f_w2, f_b2, f_w3, h_bias, lam_q1, lam_k1, lam_q2, lam_k2, subln_g, w_out, ln1_g, ln1_b,
           w_route_group, w_route_expert, w_gate, w_up, w_down, ln2_g, ln2_b):
    p = dict(ln_emb_g=ln_emb_g, ln_emb_b=ln_emb_b, rel_table=rel_table, w_in=w_in, conv_w=conv_w,
             conv_b=conv_b, f_w1=f_w1, f_b1=f_b1, f_freq=f_freq, f_w2=f_w2, f_b2=f_b2, f_w3=f_w3,
             h_bias=h_bias, lam_q1=lam_q1, lam_k1=lam_k1, lam_q2=lam_q2, lam_k2=lam_k2,
             subln_g=subln_g, w_out=w_out, ln1_g=ln1_g, ln1_b=ln1_b, w_route_group=w_route_group,
             w_route_expert=w_route_expert, w_gate=w_gate, w_up=w_up, w_down=w_down, ln2_g=ln2_g,
             ln2_b=ln2_b)
    return _trunk(x_prompt, x_sample, p)
```

```python
import functools
import math

import jax
import jax.numpy as jnp
from jax import lax
from jax.experimental import pallas as pl
from jax.experimental.pallas import tpu as pltpu

F32 = jnp.float32
BF16 = jnp.bfloat16

LN_EPS = 1e-5
REL_MAX_DIST = 128
TOP_K = 2
DECAY_TARGET = 1e-2
FAST_DECAY_PCT = 0.3
SLOW_DECAY_PCT = 1.5

LANES = 128
BF16_SUBLANES = 16
LOG2E = 1.4426950408889634
FFT_N2 = 128
VMEM_LIMIT_BYTES = 56 * 1024 * 1024


def _cparams(sem, flags=None):
    return pltpu.CompilerParams(dimension_semantics=sem, vmem_limit_bytes=VMEM_LIMIT_BYTES, flags=flags)


def _dot(a, b):
    return jnp.dot(a, b, preferred_element_type=F32)


def _layer_norm_rows(z, g, b):
    mu = jnp.mean(z, axis=-1, keepdims=True)
    zc = z - mu
    var = jnp.mean(zc * zc, axis=-1, keepdims=True)
    return zc * lax.rsqrt(var + LN_EPS) * g + b


def _ln_kernel(xa_ref, xb_ref, g_ref, b_ref, o_ref, *, na):
    i = pl.program_id(0)

    @pl.when(i < na)
    def _():
        o_ref[...] = _layer_norm_rows(xa_ref[...], g_ref[...], b_ref[...])

    @pl.when(i >= na)
    def _():
        o_ref[...] = _layer_norm_rows(xb_ref[...], g_ref[...], b_ref[...])


def _ln_embed(xa, xb, g, b, tm):
    Ta, D = xa.shape
    Tb = xb.shape[0]
    na = Ta // tm
    nb = Tb // tm
    return pl.pallas_call(
        functools.partial(_ln_kernel, na=na),
        grid=(na + nb,),
        in_specs=[
            pl.BlockSpec((tm, D), lambda i: (jnp.minimum(i, na - 1), 0)),
            pl.BlockSpec((tm, D), lambda i: (jnp.maximum(i - na, 0), 0)),
            pl.BlockSpec((1, D), lambda i: (0, 0)),
            pl.BlockSpec((1, D), lambda i: (0, 0)),
        ],
        out_specs=pl.BlockSpec((tm, D), lambda i: (i, 0)),
        out_shape=jax.ShapeDtypeStruct((Ta + Tb, D), F32),
        compiler_params=_cparams(("arbitrary",)),
        name="ln_embed",
    )(xa, xb, g.reshape(1, D), b.reshape(1, D))


def _proj_in_kernel(x_ref, wh_ref, wk_ref, wqt_ref, wvt_ref, uh_ref, k_ref, qt_ref, vt_ref):
    xb = x_ref[...].astype(BF16)
    uh_ref[...] = _dot(xb, wh_ref[...]).astype(BF16)
    k_ref[...] = _dot(xb, wk_ref[...]).astype(BF16)
    nt = (((1,), (1,)), ((), ()))
    qt_ref[...] = lax.dot_general(wqt_ref[...], xb, nt, preferred_element_type=F32).astype(BF16)
    vt = lax.dot_general(wvt_ref[...], xb, nt, preferred_element_type=F32).astype(BF16)
    n_heads, dvp, tm = vt_ref.shape
    dv = vt.shape[0] // n_heads
    pad_rows = lax.broadcasted_iota(jnp.int32, (dvp - dv, tm), 0)
    ones_pad = jnp.where(pad_rows == 0, 1.0, 0.0).astype(BF16)
    for h in range(n_heads):
        vt_ref[h, 0:dv, :] = vt[h * dv:(h + 1) * dv]
        vt_ref[h, dv:dvp, :] = ones_pad


def _proj_in(x2d, wh, wk, wqt, wvt, B, L, H, tm):
    T, D = x2d.shape
    Ch = wh.shape[1]
    Da = wk.shape[1]
    nl = L // tm
    dvp = Da // H + BF16_SUBLANES
    return pl.pallas_call(
        _proj_in_kernel,
        grid=(B, nl),
        in_specs=[
            pl.BlockSpec((tm, D), lambda b, i: (b * nl + i, 0)),
            pl.BlockSpec((D, Ch), lambda b, i: (0, 0)),
            pl.BlockSpec((D, Da), lambda b, i: (0, 0)),
            pl.BlockSpec((Da, D), lambda b, i: (0, 0)),
            pl.BlockSpec((Da, D), lambda b, i: (0, 0)),
        ],
        out_specs=[
            pl.BlockSpec((tm, Ch), lambda b, i: (b * nl + i, 0)),
            pl.BlockSpec((tm, Da), lambda b, i: (b * nl + i, 0)),
            pl.BlockSpec((None, Da, tm), lambda b, i: (b, 0, i)),
            pl.BlockSpec((None, H, dvp, tm), lambda b, i: (b, 0, 0, i)),
        ],
        out_shape=[
            jax.ShapeDtypeStruct((T, Ch), BF16),
            jax.ShapeDtypeStruct((T, Da), BF16),
            jax.ShapeDtypeStruct((B, Da, L), BF16),
            jax.ShapeDtypeStruct((B, H, dvp, L), BF16),
        ],
        compiler_params=_cparams(("arbitrary", "arbitrary")),
        name="proj_in",
    )(x2d, wh, wk, wqt, wvt)


def _hyena_pre_kernel(u_ref, up_ref, un_ref, w_ref, b_ref, vx_ref, x0_ref, *, dh, halo):
    i = pl.program_id(1)
    last = pl.num_programs(1) - 1
    u = u_ref[...].astype(F32)
    tm = u.shape[0]
    prev_row = up_ref[halo - 1:halo, :].astype(F32)
    next_row = un_ref[0:1, :].astype(F32)
    prev_row = jnp.where(i == 0, 0.0, prev_row)
    next_row = jnp.where(i == last, 0.0, next_row)
    rows = lax.broadcasted_iota(jnp.int32, (tm, 1), 0)
    um1 = jnp.where(rows == 0, prev_row, pltpu.roll(u, 1, axis=0))
    up1 = jnp.where(rows == tm - 1, next_row, pltpu.roll(u, tm - 1, axis=0))
    w = w_ref[...]
    y = b_ref[...] + um1 * w[0:1, :]
    y = y + u * w[1:2, :]
    y = y + up1 * w[2:3, :]
    x0_ref[...] = y[:, :dh].astype(BF16)
    vx_ref[...] = y[:, 2 * dh:] * y[:, dh:2 * dh]


def _hyena_pre(uh, conv_w, conv_b, B, L, tm):
    T, Ch = uh.shape
    dh = Ch // 3
    halo = 16
    nl = L // tm
    r = tm // halo
    nhalo = T // halo
    return pl.pallas_call(
        functools.partial(_hyena_pre_kernel, dh=dh, halo=halo),
        grid=(B, nl),
        in_specs=[
            pl.BlockSpec((tm, Ch), lambda b, i: (b * nl + i, 0)),
            pl.BlockSpec((halo, Ch), lambda b, i: (jnp.maximum((b * nl + i) * r - 1, 0), 0)),
            pl.BlockSpec((halo, Ch), lambda b, i: (jnp.minimum((b * nl + i + 1) * r, nhalo - 1), 0)),
            pl.BlockSpec((3, Ch), lambda b, i: (0, 0)),
            pl.BlockSpec((1, Ch), lambda b, i: (0, 0)),
        ],
        out_specs=[
            pl.BlockSpec((tm, dh), lambda b, i: (b * nl + i, 0)),
            pl.BlockSpec((tm, dh), lambda b, i: (b * nl + i, 0)),
        ],
        out_shape=[
            jax.ShapeDtypeStruct((T, dh), F32),
            jax.ShapeDtypeStruct((T, dh), BF16),
        ],
        compiler_params=_cparams(("arbitrary", "arbitrary")),
        name="hyena_pre",
    )(uh, uh, uh, conv_w, conv_b.reshape(1, Ch))


def _dft_tables(L):
    N = 2 * L
    N1 = N // FFT_N2
    NB = L // FFT_N2
    r1 = jnp.arange(N1, dtype=jnp.int32)
    r2 = jnp.arange(FFT_N2, dtype=jnp.int32)

    def cos_sin(prod, period):
        ang = (prod % period).astype(F32) * (2.0 * math.pi / period)
        return jnp.cos(ang), jnp.sin(ang)

    c1, s1 = cos_sin(r1[:, None] * r1[None, :], N1)
    f1 = jnp.concatenate([c1, -s1], axis=0).astype(BF16)
    f1i = jnp.concatenate([c1[:NB], -s1[:NB]], axis=1).astype(BF16)
    ct, st = cos_sin(r1[:, None] * r2[None, :], N)
    c2, s2 = cos_sin(r2[:, None] * r2[None, :], FFT_N2)
    f2 = jnp.concatenate([c2, -s2], axis=1).astype(BF16)
    f2i = jnp.concatenate([jnp.concatenate([c2, s2], 1),
                           jnp.concatenate([-s2, c2], 1)], 0).astype(BF16)
    return dict(f1=f1, f1d=f1[:, :NB], f1i=f1i, tr=ct, ti=-st, f2=f2, f2i=f2i)


FFT_GROUP = 8


def _dft2_forward(xs, f1_ref, tr, ti, f2_ref):
    n1 = tr.shape[0]
    a = _dot(f1_ref[...], jnp.concatenate(xs, axis=1))
    rows = []
    for j in range(len(xs)):
        aj = a[:, j * FFT_N2:(j + 1) * FFT_N2]
        ar, ai = aj[:n1], aj[n1:]
        rows += [ar * tr - ai * ti, ar * ti + ai * tr]
    p = _dot(jnp.concatenate(rows, axis=0).astype(BF16), f2_ref[...])
    out = []
    for j in range(len(xs)):
        pr, pi = p[2 * j * n1:(2 * j + 1) * n1], p[(2 * j + 1) * n1:(2 * j + 2) * n1]
        out.append((pr[:, :FFT_N2] - pi[:, FFT_N2:], pr[:, FFT_N2:] + pi[:, :FFT_N2]))
    return out


def _filter_spec_kernel(kf_ref, f1_ref, tr_ref, ti_ref, f2_ref, h_ref, *, inv_n, unroll):
    tr, ti = tr_ref[...], ti_ref[...]

    def group(g, carry):
        c0 = g * FFT_GROUP
        xs = [kf_ref[c0 + j].astype(BF16) for j in range(FFT_GROUP)]
        for j, (sr, si) in enumerate(_dft2_forward(xs, f1_ref, tr, ti, f2_ref)):
            h_ref[c0 + j] = (jnp.concatenate([sr, si], axis=1) * inv_n).astype(BF16)
        return carry

    lax.fori_loop(0, kf_ref.shape[0] // FFT_GROUP, group, 0, unroll=unroll)


def _filter_spec(kf, tb, cc):
    Dh, n1, _ = kf.shape
    fix2 = lambda c: (0, 0)
    return pl.pallas_call(
        functools.partial(_filter_spec_kernel, inv_n=1.0 / (n1 * FFT_N2), unroll=2),
        grid=(Dh // cc,),
        in_specs=[
            pl.BlockSpec((cc, n1, FFT_N2), lambda c: (c, 0, 0)),
            pl.BlockSpec((2 * n1, n1), fix2),
            pl.BlockSpec((n1, FFT_N2), fix2),
            pl.BlockSpec((n1, FFT_N2), fix2),
            pl.BlockSpec((FFT_N2, 2 * FFT_N2), fix2),
        ],
        out_specs=pl.BlockSpec((cc, n1, 2 * FFT_N2), lambda c: (c, 0, 0)),
        out_shape=jax.ShapeDtypeStruct((Dh, n1, 2 * FFT_N2), BF16),
        compiler_params=_cparams(("arbitrary",)),
        name="filter_spec",
    )(kf, tb["f1"], tb["tr"], tb["ti"], tb["f2"])


def _fft_conv_kernel(x_ref, f1_ref, f1i_ref, tr_ref, ti_ref, f2_ref, f2i_ref, h_ref, o_ref, *, unroll):
    tr, ti = tr_ref[...], ti_ref[...]
    n1 = tr.shape[0]

    def group(g, carry):
        c0 = g * FFT_GROUP
        xs = [x_ref[c0 + j].astype(BF16) for j in range(FFT_GROUP)]
        ys = []
        for j, (sr, si) in enumerate(_dft2_forward(xs, f1_ref, tr, ti, f2_ref)):
            h = h_ref[c0 + j].astype(F32)
            hr, hi = h[:, :FFT_N2], h[:, FFT_N2:]
            ys.append(jnp.concatenate([sr * hr - si * hi, sr * hi + si * hr], axis=1))
        b = _dot(jnp.concatenate(ys, axis=0).astype(BF16), f2i_ref[...])
        cols = []
        for j in range(FFT_GROUP):
            bj = b[j * n1:(j + 1) * n1]
            br, bi = bj[:, :FFT_N2], bj[:, FFT_N2:]
            cols.append(jnp.concatenate([br * tr + bi * ti, bi * tr - br * ti], axis=0))
        y = _dot(f1i_ref[...], jnp.concatenate(cols, axis=1).astype(BF16))
        for j in range(FFT_GROUP):
            o_ref[c0 + j] = y[:, j * FFT_N2:(j + 1) * FFT_N2]
        return carry

    lax.fori_loop(0, x_ref.shape[0] // FFT_GROUP, group, 0, unroll=unroll)


def _fft_conv(x4, tb, h_spec, cc):
    B, Dh, nb, _ = x4.shape
    n1 = 2 * nb
    fix2 = lambda c, b: (0, 0)
    return pl.pallas_call(
        functools.partial(_fft_conv_kernel, unroll=2),
        grid=(Dh // cc, B),
        in_specs=[
            pl.BlockSpec((None, cc, nb, FFT_N2), lambda c, b: (b, c, 0, 0)),
            pl.BlockSpec((2 * n1, nb), fix2),
            pl.BlockSpec((nb, 2 * n1), fix2),
            pl.BlockSpec((n1, FFT_N2), fix2),
            pl.BlockSpec((n1, FFT_N2), fix2),
            pl.BlockSpec((FFT_N2, 2 * FFT_N2), fix2),
            pl.BlockSpec((2 * FFT_N2, 2 * FFT_N2), fix2),
            pl.BlockSpec((cc, n1, 2 * FFT_N2), lambda c, b: (c, 0, 0)),
        ],
        out_specs=pl.BlockSpec((None, cc, nb, FFT_N2), lambda c, b: (b, c, 0, 0)),
        out_shape=jax.ShapeDtypeStruct((B, Dh, nb, FFT_N2), F32),
        compiler_params=_cparams(("arbitrary", "arbitrary")),
        name="fft_conv",
    )(x4, tb["f1d"], tb["f1i"], tb["tr"], tb["ti"], tb["f2"], tb["f2i"], h_spec)


def _bias_tiles_kernel(table_ref, o_ref, *, tk, tq, n_buckets):
    h = pl.program_id(0)
    c = pl.program_id(1)
    nb = n_buckets // 2
    max_exact = nb // 2
    kv = lax.broadcasted_iota(jnp.int32, (tk, tq), 0)
    q = lax.broadcasted_iota(jnp.int32, (tk, tq), 1)
    rel = (c - 2) * tk + kv - q
    ret = jnp.where(rel > 0, nb, 0)
    n = jnp.abs(rel)
    large = max_exact + (jnp.log(jnp.maximum(n, 1).astype(F32) / max_exact)
                         / math.log(REL_MAX_DIST / max_exact) * (nb - max_exact)).astype(jnp.int32)
    large = jnp.minimum(large, nb - 1)
    bucket = ret + jnp.where(n < max_exact, n, large)
    out = jnp.zeros((tk, tq), F32)
    for b in range(n_buckets):
        out = jnp.where(bucket == b, table_ref[b, h], out)
    o_ref[...] = (out * LOG2E).astype(o_ref.dtype)


def _bias_tiles(rel_table, tk, tq):
    n_buckets, H = rel_table.shape
    n_cls = tq // tk + 4
    return pl.pallas_call(
        functools.partial(_bias_tiles_kernel, tk=tk, tq=tq, n_buckets=n_buckets),
        grid=(H, n_cls),
        in_specs=[pl.BlockSpec(memory_space=pltpu.SMEM)],
        out_specs=pl.BlockSpec((None, None, tk, tq), lambda h, c: (h, c, 0, 0)),
        out_shape=jax.ShapeDtypeStruct((H, n_cls, tk, tq), BF16),
        compiler_params=_cparams(("arbitrary", "arbitrary")),
        name="bias_tiles",
    )(rel_table)


def _attn_kernel(lam_ref, qt_ref, k_ref, vt_ref, bias_ref, g_ref, o_ref, acc1_ref, acc2_ref,
                 sa_ref, sb_ref, *, tk, tq, nk, hd, out_scale):
    qi = pl.program_id(2)
    dv = 2 * hd
    ratio = tq // tk
    qt = qt_ref[...]
    half = lax.broadcasted_iota(jnp.int32, qt.shape, 0) < hd
    zero = jnp.zeros_like(qt)
    q1 = jnp.where(half, qt, zero)
    q2 = jnp.where(half, zero, qt)
    acc1_ref[...] = jnp.zeros_like(acc1_ref)
    acc2_ref[...] = jnp.zeros_like(acc2_ref)

    def scores(ki, s_ref):
        off = pl.multiple_of(ki * tk, tk)
        kb = k_ref[pl.ds(off, tk), :]
        bias = bias_ref[jnp.clip(ki - ratio * qi, -2, ratio + 1) + 2]
        s1 = _dot(kb, q1).astype(BF16) + bias
        s_ref[0] = s1
        s2 = _dot(kb, q2).astype(BF16) + bias
        s_ref[1] = s2
        return (jnp.max(s1, axis=0, keepdims=True).astype(F32),
                jnp.max(s2, axis=0, keepdims=True).astype(F32))

    def accumulate(ki, s_ref, mt, m):
        off = pl.multiple_of(ki * tk, tk)
        vb = vt_ref[:, pl.ds(off, tk)]
        out = []
        for j, acc_ref in enumerate((acc1_ref, acc2_ref)):
            m_new = jnp.maximum(m[j], mt[j])
            alpha = jnp.exp2(m[j] - m_new)
            p = jnp.exp2(s_ref[j] - m_new.astype(BF16))
            acc_ref[...] = alpha * acc_ref[...] + _dot(vb, p)
            out.append(m_new)
        return tuple(out)

    def pair(j, carry):
        mt_a, m = carry
        ka = 2 * j
        mt_b = scores(ka + 1, sb_ref)
        m = accumulate(ka, sa_ref, mt_a, m)
        mt_a = scores(jnp.minimum(ka + 2, nk - 1), sa_ref)
        m = accumulate(ka + 1, sb_ref, mt_b, m)
        return mt_a, m

    neg = jnp.full((1, tq), -jnp.inf, F32)
    mt0 = scores(0, sa_ref)
    _, (m1, m2) = lax.fori_loop(0, nk // 2, pair, (mt0, (neg, neg)), unroll=2)
    a1 = acc1_ref[...]
    a2 = acc2_ref[...]
    o = a1[:dv] / a1[dv:dv + 1] - lam_ref[0] * (a2[:dv] / a2[dv:dv + 1])
    ms = jnp.mean(o * o, axis=0, keepdims=True)
    o = o * lax.rsqrt(ms + LN_EPS) * g_ref[...] * out_scale
    o_ref[...] = o.T.astype(BF16)


def _attention(lam, qt, k, vt, bias, g_col, B, L, H, hd, tk, tq, lam_init):
    Da = H * 2 * hd
    nq = L // tq
    nk = L // tk
    n_cls = bias.shape[1]
    assert nk % 2 == 0 and tq % tk == 0
    dvp = vt.shape[2]
    k3 = k.reshape(B, L, Da)
    out = pl.pallas_call(
        functools.partial(_attn_kernel, tk=tk, tq=tq, nk=nk, hd=hd, out_scale=1.0 - lam_init),
        grid=(B, H, nq),
        in_specs=[
            pl.BlockSpec(memory_space=pltpu.SMEM),
            pl.BlockSpec((None, 2 * hd, tq), lambda b, h, q: (b, h, q)),
            pl.BlockSpec((None, L, 2 * hd), lambda b, h, q: (b, 0, h)),
            pl.BlockSpec((None, None, dvp, L), lambda b, h, q: (b, h, 0, 0)),
            pl.BlockSpec((None, n_cls, tk, tq), lambda b, h, q: (h, 0, 0, 0)),
            pl.BlockSpec((2 * hd, 1), lambda b, h, q: (0, 0)),
        ],
        out_specs=pl.BlockSpec((None, tq, 2 * hd), lambda b, h, q: (b, q, h)),
        out_shape=jax.ShapeDtypeStruct((B, L, Da), BF16),
        scratch_shapes=[pltpu.VMEM((dvp, tq), F32), pltpu.VMEM((dvp, tq), F32),
                        pltpu.VMEM((2, tk, tq), BF16), pltpu.VMEM((2, tk, tq), BF16)],
        compiler_params=_cparams(("arbitrary", "arbitrary", "arbitrary")),
        name="diff_attn",
    )(lam, qt, k3, vt, bias, g_col)
    return out.reshape(B * L, Da)


def _proj_out_kernel(cy_ref, vx_ref, x0_ref, hb_ref, ya_ref, x_ref, woh_ref, woa_ref, g_ref, b_ref,
                     wrc_ref, wrh_ref, x1_ref, eid_ref, gate_ref, *, alpha, n_groups, per_group):
    yh = ((cy_ref[...] + vx_ref[...] * hb_ref[...]) * x0_ref[...].astype(F32)).astype(BF16)
    mix = _dot(yh, woh_ref[...]) + _dot(ya_ref[...], woa_ref[...])
    x1 = _layer_norm_rows(alpha * x_ref[...] + mix, g_ref[...], b_ref[...])
    x1_ref[...] = x1
    hi = x1.astype(BF16)
    lo = (x1 - hi.astype(F32)).astype(BF16)
    lg2 = _dot(hi, wrc_ref[...])
    lg = lg2[:, :LANES] + lg2[:, LANES:] + _dot(lo, wrh_ref[...])
    col = lax.broadcasted_iota(jnp.int32, lg.shape, 1).astype(F32)
    far = jnp.float32(LANES)
    neg_inf = jnp.float32(-jnp.inf)

    def first_col(mask):
        return jnp.min(jnp.where(mask, col, far), axis=-1, keepdims=True)

    gmask = col < n_groups
    glog = jnp.where(gmask, lg, neg_inf)
    gmax = jnp.max(glog, axis=-1, keepdims=True)
    g_sel = first_col(glog == gmax)
    g_w = 1.0 / jnp.sum(jnp.where(gmask, jnp.exp(glog - gmax), 0.0), axis=-1, keepdims=True)
    lo_col = n_groups + g_sel * per_group
    emask = (col >= lo_col) & (col < lo_col + per_group)
    elog = jnp.where(emask, lg, neg_inf)
    emax = jnp.max(elog, axis=-1, keepdims=True)
    eexp = jnp.where(emask, jnp.exp(elog - emax), 0.0)
    prob = eexp / jnp.sum(eexp, axis=-1, keepdims=True)
    p1 = jnp.max(prob, axis=-1, keepdims=True)
    i1 = first_col(emask & (prob == p1))
    mask2 = emask & (col != i1)
    p2 = jnp.max(jnp.where(mask2, prob, -1.0), axis=-1, keepdims=True)
    i2 = first_col(mask2 & (prob == p2))
    psum = p1 + p2
    eid = jnp.where(col == 0, i1 - n_groups, jnp.where(col == 1, i2 - n_groups, 0.0))
    eid_ref[...] = eid.astype(jnp.int32)
    gate_ref[...] = jnp.where(col == 0, g_w * p1 / psum, jnp.where(col == 1, g_w * p2 / psum, 0.0))


def _proj_out(cy, vx, x0c, hb, ya, x, woh, woa, g, b, wrc, wrh, alpha, n_groups, per_group, tm):
    T, D = x.shape
    Dh = cy.shape[1]
    Da = ya.shape[1]
    row = lambda i: (i, 0)
    fix = lambda i: (0, 0)
    return pl.pallas_call(
        functools.partial(_proj_out_kernel, alpha=alpha, n_groups=n_groups, per_group=per_group),
        grid=(T // tm,),
        in_specs=[
            pl.BlockSpec((tm, Dh), row),
            pl.BlockSpec((tm, Dh), row),
            pl.BlockSpec((tm, Dh), row),
            pl.BlockSpec((1, Dh), fix),
            pl.BlockSpec((tm, Da), row),
            pl.BlockSpec((tm, D), row),
            pl.BlockSpec((Dh, D), fix),
            pl.BlockSpec((Da, D), fix),
            pl.BlockSpec((1, D), fix),
            pl.BlockSpec((1, D), fix),
            pl.BlockSpec((D, 2 * LANES), fix),
            pl.BlockSpec((D, LANES), fix),
        ],
        out_specs=[
            pl.BlockSpec((tm, D), row),
            pl.BlockSpec((tm, LANES), row),
            pl.BlockSpec((tm, LANES), row),
        ],
        out_shape=[
            jax.ShapeDtypeStruct((T, D), F32),
            jax.ShapeDtypeStruct((T, LANES), jnp.int32),
            jax.ShapeDtypeStruct((T, LANES), F32),
        ],
        compiler_params=_cparams(("arbitrary",)),
        name="proj_out_ln_router",
    )(cy, vx, x0c, hb, ya, x, woh, woa, g, b, wrc, wrh)


def _moe_kernel(be_ref, nu_ref, src_ref, srcn_ref, dst_ref, x_hbm, wg_ref, wu_ref, wd_ref, out_hbm,
                xbuf, ybuf, wgb, wub, wdb, sem_in, sem_out, *, bm, chunk):
    i = pl.program_id(0)
    n_used = nu_ref[0]
    slot = i % 2

    def gather_row(idx_ref, s, r):
        return pltpu.make_async_copy(x_hbm.at[pl.ds(idx_ref[0, r], 1), :],
                                     xbuf.at[s, pl.ds(r, 1), :], sem_in.at[s])

    def scatter_row(s, r):
        return pltpu.make_async_copy(ybuf.at[s, pl.ds(r, 1), :],
                                     out_hbm.at[pl.ds(dst_ref[0, r], 1), :], sem_out.at[s])

    def wait_gather(s):
        pltpu.make_async_copy(x_hbm.at[pl.ds(0, bm), :], xbuf.at[s], sem_in.at[s]).wait()

    def wait_scatter(s):
        pltpu.make_async_copy(ybuf.at[s], out_hbm.at[pl.ds(0, bm), :], sem_out.at[s]).wait()

    @pl.when(i == 0)
    def _():
        def body(r, c):
            gather_row(src_ref, 0, r).start()
            return c

        lax.fori_loop(0, bm, body, 0, unroll=8)

    def run(cur, nxt):
        changed = jnp.logical_or(i == 0, be_ref[i] != be_ref[jnp.maximum(i - 1, 0)])

        @pl.when(changed)
        def _():
            wgb[...] = wg_ref[...].astype(BF16)
            wub[...] = wu_ref[...].astype(BF16)
            wdb[...] = wd_ref[...].astype(BF16)

        wait_gather(cur)
        for r in range(bm):
            gather_row(srcn_ref, nxt, r).start()
        xb = xbuf[cur].astype(BF16)
        hg = _dot(xb, wgb[...])
        hu = _dot(xb, wub[...])
        act = ((hg / (1.0 + jnp.exp(-hg))) * hu).astype(BF16)
        for c in range(bm // chunk):
            rows = slice(c * chunk, (c + 1) * chunk)
            ybuf[cur, rows, :] = _dot(act[rows], wdb[...])
            for r in range(c * chunk, (c + 1) * chunk):
                scatter_row(cur, r).start()

        @pl.when(i > 0)
        def _():
            wait_scatter(nxt)

        @pl.when(i == n_used - 1)
        def _():
            wait_scatter(cur)
            wait_gather(nxt)

    for parity in (0, 1):
        @pl.when(jnp.logical_and(i < n_used, slot == parity))
        def _():
            run(parity, 1 - parity)


def _moe_experts(block_e, n_used, src, dst, x1, w_gate, w_up, w_down, n_rows_out, bm):
    n_blocks = block_e.shape[0]
    T, D = x1.shape
    E, _, De = w_gate.shape
    idx_block = (None, 1, bm)
    grid_spec = pltpu.PrefetchScalarGridSpec(
        num_scalar_prefetch=2,
        grid=(n_blocks,),
        in_specs=[
            pl.BlockSpec(idx_block, lambda i, be, nu: (i, 0, 0), memory_space=pltpu.SMEM),
            pl.BlockSpec(idx_block, lambda i, be, nu: (jnp.minimum(i + 1, n_blocks - 1), 0, 0),
                         memory_space=pltpu.SMEM),
            pl.BlockSpec(idx_block, lambda i, be, nu: (i, 0, 0), memory_space=pltpu.SMEM),
            pl.BlockSpec(memory_space=pl.ANY),
            pl.BlockSpec((None, D, De), lambda i, be, nu: (be[i], 0, 0)),
            pl.BlockSpec((None, D, De), lambda i, be, nu: (be[i], 0, 0)),
            pl.BlockSpec((None, De, D), lambda i, be, nu: (be[i], 0, 0)),
        ],
        out_specs=pl.BlockSpec(memory_space=pl.ANY),
        scratch_shapes=[
            pltpu.VMEM((2, bm, D), F32),
            pltpu.VMEM((2, bm, D), F32),
            pltpu.VMEM((D, De), BF16),
            pltpu.VMEM((D, De), BF16),
            pltpu.VMEM((De, D), BF16),
            pltpu.SemaphoreType.DMA((2,)),
            pltpu.SemaphoreType.DMA((2,)),
        ],
    )
    return pl.pallas_call(
        functools.partial(_moe_kernel, bm=bm, chunk=min(bm, 128)),
        grid_spec=grid_spec,
        out_shape=jax.ShapeDtypeStruct((n_rows_out, D), F32),
        compiler_params=_cparams(("arbitrary",)),
        name="moe_experts",
    )(block_e, n_used, src, src, dst, x1, w_gate, w_up, w_down)


def _combine_kernel(ya_ref, yb_ref, gate_ref, x1_ref, g_ref, b_ref, *o_refs, alpha, na):
    gates = gate_ref[...]
    y = gates[:, 0:1] * ya_ref[...] + gates[:, 1:2] * yb_ref[...]
    res = _layer_norm_rows(alpha * x1_ref[...] + y, g_ref[...], b_ref[...])
    if len(o_refs) == 1:
        o_refs[0][...] = res
    else:
        i = pl.program_id(0)

        @pl.when(i < na)
        def _():
            o_refs[0][...] = res

        @pl.when(i >= na)
        def _():
            o_refs[1][...] = res


def _combine_ln(y2, gates, x1, g, b, alpha, tm, split_rows=None):
    T, D = x1.shape
    nt = T // tm
    row = lambda i: (i, 0)
    fix = lambda i: (0, 0)
    if split_rows is None:
        na = nt
        out_specs = pl.BlockSpec((tm, D), row)
        out_shape = jax.ShapeDtypeStruct((T, D), F32)
    else:
        na = split_rows // tm
        out_specs = [pl.BlockSpec((tm, D), lambda i: (jnp.minimum(i, na - 1), 0)),
                     pl.BlockSpec((tm, D), lambda i: (jnp.maximum(i - na, 0), 0))]
        out_shape = [jax.ShapeDtypeStruct((split_rows, D), F32),
                     jax.ShapeDtypeStruct((T - split_rows, D), F32)]
    return pl.pallas_call(
        functools.partial(_combine_kernel, alpha=alpha, na=na),
        grid=(nt,),
        in_specs=[
            pl.BlockSpec((tm, D), row),
            pl.BlockSpec((tm, D), lambda i: (nt + i, 0)),
            pl.BlockSpec((tm, LANES), row),
            pl.BlockSpec((tm, D), row),
            pl.BlockSpec((1, D), fix),
            pl.BlockSpec((1, D), fix),
        ],
        out_specs=out_specs,
        out_shape=out_shape,
        compiler_params=_cparams(("arbitrary",)),
        name="moe_combine_ln",
    )(y2, y2, gates, x1, g, b)


def _hyena_filter(L, dh, f_w1, f_b1, f_freq, f_w2, f_b2, f_w3):
    emb = f_w1.shape[0]
    bands = (emb - 1) // 2
    t = jnp.linspace(0.0, 1.0, L, dtype=F32)[:, None]
    w = (2.0 * math.pi / L) * jnp.arange(L, dtype=F32)[:, None]
    f = jnp.linspace(1e-4, bands - 1, bands, dtype=F32)[None, :]
    z = jnp.concatenate([t, jnp.cos(f * w), -jnp.sin(f * w)], axis=-1)
    hp = lax.Precision.HIGHEST
    max_decay = math.log(DECAY_TARGET) / FAST_DECAY_PCT
    min_decay = math.log(DECAY_TARGET) / SLOW_DECAY_PCT
    deltas = jnp.abs(jnp.linspace(min_decay, max_decay, dh, dtype=F32))

    def taps(zz, tt, w3):
        h = jnp.sin(f_freq * (jnp.dot(zz, f_w1, precision=hp) + f_b1))
        h = jnp.sin(f_freq * (jnp.dot(h, f_w2, precision=hp) + f_b2))
        return jnp.dot(h, w3, precision=hp) * jnp.exp(-tt * deltas)

    h_fwd = taps(z, t, f_w3[:, :dh])
    h_bwd_rev = taps(z[::-1], t[::-1], f_w3[:, dh:])[:L - 1]
    k_full = jnp.concatenate([h_fwd, jnp.zeros((1, dh), F32), h_bwd_rev], axis=0)
    return k_full / jnp.sum(jnp.abs(k_full), axis=0, keepdims=True)


def _dispatch_tables(eid, n_experts, bm):
    T = eid.shape[0]
    A = T * TOP_K
    flat_e = eid.reshape(A)
    se, order = lax.sort((flat_e, jnp.arange(A, dtype=jnp.int32)), num_keys=1, is_stable=True)
    experts = jnp.arange(n_experts, dtype=jnp.int32)
    onehot = se[:, None] == experts[None, :]
    counts = jnp.sum(onehot, axis=0, dtype=jnp.int32)
    starts = jnp.cumsum(counts) - counts
    pcounts = (counts + bm - 1) // bm * bm
    pends = jnp.cumsum(pcounts)
    pstarts = pends - pcounts
    n_blocks = -(-A // bm) + n_experts
    P = n_blocks * bm
    first_row = jnp.arange(n_blocks, dtype=jnp.int32) * bm
    block_e = jnp.minimum(jnp.sum(pends[None, :] <= first_row[:, None], axis=1, dtype=jnp.int32),
                          n_experts - 1)
    shift = jnp.sum(jnp.where(onehot, (pstarts - starts)[None, :], 0), axis=1)
    dest = jnp.arange(A, dtype=jnp.int32) + shift
    row_asg = jnp.full((P,), -1, jnp.int32).at[dest].set(order, unique_indices=True,
                                                         indices_are_sorted=True)
    pad = row_asg < 0
    src = jnp.maximum(row_asg, 0) // TOP_K
    dst = jnp.where(pad, A + jnp.cumsum(pad.astype(jnp.int32)) - 1, (row_asg % TOP_K) * T + row_asg // TOP_K)
    n_used = (pends[-1] // bm).astype(jnp.int32).reshape(1)
    return block_e, n_used, src.reshape(n_blocks, 1, bm), dst.astype(jnp.int32).reshape(n_blocks, 1, bm), P


def _pick(n, pref):
    t = min(n, pref)
    assert n % t == 0, (n, t)
    return t


def _trunk(xa, xb, p):
    Ba, L, D = xa.shape
    Bb = xb.shape[0]
    assert xb.shape[1:] == (L, D)
    B = Ba + Bb
    T = B * L
    depth = p["w_in"].shape[0]
    dh = p["h_bias"].shape[1]
    H = p["rel_table"].shape[1]
    hd = p["lam_q1"].shape[1]
    da = H * 2 * hd
    n_groups = p["w_route_group"].shape[2]
    n_experts = p["w_route_expert"].shape[2]
    per_group = n_experts // n_groups
    alpha = (2.0 * depth) ** 0.25
    assert L % FFT_N2 == 0 and (L // FFT_N2) % 8 == 0
    assert 2 * hd == LANES and dh % LANES == 0 and n_groups + n_experts <= LANES

    tm = _pick(L, 512)
    tk_attn = _pick(L, 512)
    tq_attn = _pick(L, 1024)
    bm = 512

    xs = _ln_embed(xa.reshape(Ba * L, D), xb.reshape(Bb * L, D), p["ln_emb_g"], p["ln_emb_b"], tm)
    tables = _dft_tables(L)
    nb = L // FFT_N2
    bias = _bias_tiles(p["rel_table"], tk_attn, tq_attn)

    for l in range(depth):
        lam_init = 0.8 - 0.6 * math.exp(-0.3 * l)
        w_in = p["w_in"][l]
        wh = w_in[:, :3 * dh].astype(BF16)
        wa = w_in[:, 3 * dh:]
        wqt = (wa[:, :da] * (hd ** -0.5 * LOG2E)).T.astype(BF16)
        wk = wa[:, da:2 * da].astype(BF16)
        wvt = wa[:, 2 * da:].T.astype(BF16)
        uh, k, qt, vt = _proj_in(xs, wh, wk, wqt, wvt, B, L, H, tm)

        vx, x0c = _hyena_pre(uh, p["conv_w"][l], p["conv_b"][l], B, L, tm)
        k_full = _hyena_filter(L, dh, p["f_w1"][l], p["f_b1"][l], p["f_freq"][l], p["f_w2"][l],
                               p["f_b2"][l], p["f_w3"][l])
        h_spec = _filter_spec(k_full.T.reshape(dh, 2 * nb, FFT_N2), tables, LANES)
        vx4 = vx.reshape(B, L, dh).transpose(0, 2, 1).reshape(B, dh, nb, FFT_N2)
        cy = _fft_conv(vx4, tables, h_spec, LANES)
        cy = cy.reshape(B, dh, L).transpose(0, 2, 1).reshape(T, dh)

        lam = (jnp.exp(jnp.sum(p["lam_q1"][l] * p["lam_k1"][l]))
               - jnp.exp(jnp.sum(p["lam_q2"][l] * p["lam_k2"][l])) + lam_init).reshape(1).astype(F32)
        ya = _attention(lam, qt, k, vt, bias, p["subln_g"][l].reshape(2 * hd, 1), B, L, H, hd, tk_attn,
                        tq_attn, lam_init)

        w_out = p["w_out"][l]
        w_r = jnp.concatenate([p["w_route_group"][l], p["w_route_expert"][l]], axis=1)
        w_r = jnp.pad(w_r, ((0, 0), (0, LANES - w_r.shape[1])))
        w_r_hi = w_r.astype(BF16)
        w_r_lo = (w_r - w_r_hi.astype(F32)).astype(BF16)
        x1, eid, gates = _proj_out(
            cy, vx, x0c, p["h_bias"][l].reshape(1, dh), ya, xs,
            w_out[:dh].astype(BF16), w_out[dh:].astype(BF16),
            p["ln1_g"][l].reshape(1, D), p["ln1_b"][l].reshape(1, D),
            jnp.concatenate([w_r_hi, w_r_lo], axis=1), w_r_hi,
            alpha, n_groups, per_group, tm)

        block_e, n_used, src, dst, n_rows = _dispatch_tables(eid[:, :TOP_K], n_experts, bm)
        y2 = _moe_experts(block_e, n_used, src, dst, x1, p["w_gate"][l], p["w_up"][l], p["w_down"][l],
                          n_rows, bm)
        xs = _combine_ln(y2, gates, x1, p["ln2_g"][l].reshape(1, D), p["ln2_b"][l].reshape(1, D),
                         alpha, tm, split_rows=Ba * L if l == depth - 1 else None)
    return xs[0].reshape(Ba, L, D), xs[1].reshape(Bb, L, D)


def kernel(x_prompt, x_sample, ln_emb_g, ln_emb_b, rel_table, w_in, conv_w, conv_b, f_w1, f_b1, f_freq,
           f_w2, f_b2, f_w3, h_bias, lam_q1, lam_k1, lam_q2, lam_k2, subln_g, w_out, ln1_g, ln1_b,
           w_route_group, w_route_expert, w_gate, w_up, w_down, ln2_g, ln2_b):
    p = dict(ln_emb_g=ln_emb_g, ln_emb_b=ln_emb_b, rel_table=rel_table, w_in=w_in, conv_w=conv_w,
             conv_b=conv_b, f_w1=f_w1, f_b1=f_b1, f_freq=f_freq, f_w2=f_w2, f_b2=f_b2, f_w3=f_w3,
             h_bias=h_bias, lam_q1=lam_q1, lam_k1=lam_k1, lam_q2=lam_q2, lam_k2=lam_k2,
             subln_g=subln_g, w_out=w_out, ln1_g=ln1_g, ln1_b=ln1_b, w_route_group=w_route_group,
             w_route_expert=w_route_expert, w_gate=w_gate, w_up=w_up, w_down=w_down, ln2_g=ln2_g,
             ln2_b=ln2_b)
    return _trunk(x_prompt, x_sample, p)
```

```python
import functools
import math

import jax
import jax.numpy as jnp
from jax import lax
from jax.experimental import pallas as pl
from jax.experimental.pallas import tpu as pltpu

F32 = jnp.float32
BF16 = jnp.bfloat16

LN_EPS = 1e-5
REL_MAX_DIST = 128
TOP_K = 2
DECAY_TARGET = 1e-2
FAST_DECAY_PCT = 0.3
SLOW_DECAY_PCT = 1.5

LANES = 128
BF16_SUBLANES = 16
LOG2E = 1.4426950408889634
FFT_N2 = 128
VMEM_LIMIT_BYTES = 56 * 1024 * 1024


def _cparams(sem, flags=None):
    return pltpu.CompilerParams(dimension_semantics=sem, vmem_limit_bytes=VMEM_LIMIT_BYTES, flags=flags)


def _dot(a, b):
    return jnp.dot(a, b, preferred_element_type=F32)


def _layer_norm_rows(z, g, b):
    mu = jnp.mean(z, axis=-1, keepdims=True)
    zc = z - mu
    var = jnp.mean(zc * zc, axis=-1, keepdims=True)
    return zc * lax.rsqrt(var + LN_EPS) * g + b


def _ln_kernel(xa_ref, xb_ref, g_ref, b_ref, o_ref, *, na):
    i = pl.program_id(0)

    @pl.when(i < na)
    def _():
        o_ref[...] = _layer_norm_rows(xa_ref[...], g_ref[...], b_ref[...])

    @pl.when(i >= na)
    def _():
        o_ref[...] = _layer_norm_rows(xb_ref[...], g_ref[...], b_ref[...])


def _ln_embed(xa, xb, g, b, tm):
    Ta, D = xa.shape
    Tb = xb.shape[0]
    na = Ta // tm
    nb = Tb // tm
    return pl.pallas_call(
        functools.partial(_ln_kernel, na=na),
        grid=(na + nb,),
        in_specs=[
            pl.BlockSpec((tm, D), lambda i: (jnp.minimum(i, na - 1), 0)),
            pl.BlockSpec((tm, D), lambda i: (jnp.maximum(i - na, 0), 0)),
            pl.BlockSpec((1, D), lambda i: (0, 0)),
            pl.BlockSpec((1, D), lambda i: (0, 0)),
        ],
        out_specs=pl.BlockSpec((tm, D), lambda i: (i, 0)),
        out_shape=jax.ShapeDtypeStruct((Ta + Tb, D), F32),
        compiler_params=_cparams(("arbitrary",)),
        name="ln_embed",
    )(xa, xb, g.reshape(1, D), b.reshape(1, D))


def _proj_in_kernel(x_ref, wh_ref, wk_ref, wqt_ref, wvt_ref, uh_ref, k_ref, qt_ref, vt_ref):
    xb = x_ref[...].astype(BF16)
    uh_ref[...] = _dot(xb, wh_ref[...]).astype(BF16)
    k_ref[...] = _dot(xb, wk_ref[...]).astype(BF16)
    nt = (((1,), (1,)), ((), ()))
    qt_ref[...] = lax.dot_general(wqt_ref[...], xb, nt, preferred_element_type=F32).astype(BF16)
    vt = lax.dot_general(wvt_ref[...], xb, nt, preferred_element_type=F32).astype(BF16)
    n_heads, dvp, tm = vt_ref.shape
    dv = vt.shape[0] // n_heads
    pad_rows = lax.broadcasted_iota(jnp.int32, (dvp - dv, tm), 0)
    ones_pad = jnp.where(pad_rows == 0, 1.0, 0.0).astype(BF16)
    for h in range(n_heads):
        vt_ref[h, 0:dv, :] = vt[h * dv:(h + 1) * dv]
        vt_ref[h, dv:dvp, :] = ones_pad


def _proj_in(x2d, wh, wk, wqt, wvt, B, L, H, tm):
    T, D = x2d.shape
    Ch = wh.shape[1]
    Da = wk.shape[1]
    nl = L // tm
    dvp = Da // H + BF16_SUBLANES
    return pl.pallas_call(
        _proj_in_kernel,
        grid=(B, nl),
        in_specs=[
            pl.BlockSpec((tm, D), lambda b, i: (b * nl + i, 0)),
            pl.BlockSpec((D, Ch), lambda b, i: (0, 0)),
            pl.BlockSpec((D, Da), lambda b, i: (0, 0)),
            pl.BlockSpec((Da, D), lambda b, i: (0, 0)),
            pl.BlockSpec((Da, D), lambda b, i: (0, 0)),
        ],
        out_specs=[
            pl.BlockSpec((tm, Ch), lambda b, i: (b * nl + i, 0)),
            pl.BlockSpec((tm, Da), lambda b, i: (b * nl + i, 0)),
            pl.BlockSpec((None, Da, tm), lambda b, i: (b, 0, i)),
            pl.BlockSpec((None, H, dvp, tm), lambda b, i: (b, 0, 0, i)),
        ],
        out_shape=[
            jax.ShapeDtypeStruct((T, Ch), BF16),
            jax.ShapeDtypeStruct((T, Da), BF16),
            jax.ShapeDtypeStruct((B, Da, L), BF16),
            jax.ShapeDtypeStruct((B, H, dvp, L), BF16),
        ],
        compiler_params=_cparams(("arbitrary", "arbitrary")),
        name="proj_in",
    )(x2d, wh, wk, wqt, wvt)


def _hyena_pre_kernel(u_ref, up_ref, un_ref, w_ref, b_ref, vx_ref, x0_ref, *, dh, halo):
    i = pl.program_id(1)
    last = pl.num_programs(1) - 1
    u = u_ref[...].astype(F32)
    tm = u.shape[0]
    prev_row = up_ref[halo - 1:halo, :].astype(F32)
    next_row = un_ref[0:1, :].astype(F32)
    prev_row = jnp.where(i == 0, 0.0, prev_row)
    next_row = jnp.where(i == last, 0.0, next_row)
    rows = lax.broadcasted_iota(jnp.int32, (tm, 1), 0)
    um1 = jnp.where(rows == 0, prev_row, pltpu.roll(u, 1, axis=0))
    up1 = jnp.where(rows == tm - 1, next_row, pltpu.roll(u, tm - 1, axis=0))
    w = w_ref[...]
    y = b_ref[...] + um1 * w[0:1, :]
    y = y + u * w[1:2, :]
    y = y + up1 * w[2:3, :]
    x0_ref[...] = y[:, :dh].astype(BF16)
    vx_ref[...] = y[:, 2 * dh:] * y[:, dh:2 * dh]


def _hyena_pre(uh, conv_w, conv_b, B, L, tm):
    T, Ch = uh.shape
    dh = Ch // 3
    halo = 16
    nl = L // tm
    r = tm // halo
    nhalo = T // halo
    return pl.pallas_call(
        functools.partial(_hyena_pre_kernel, dh=dh, halo=halo),
        grid=(B, nl),
        in_specs=[
            pl.BlockSpec((tm, Ch), lambda b, i: (b * nl + i, 0)),
            pl.BlockSpec((halo, Ch), lambda b, i: (jnp.maximum((b * nl + i) * r - 1, 0), 0)),
            pl.BlockSpec((halo, Ch), lambda b, i: (jnp.minimum((b * nl + i + 1) * r, nhalo - 1), 0)),
            pl.BlockSpec((3, Ch), lambda b, i: (0, 0)),
            pl.BlockSpec((1, Ch), lambda b, i: (0, 0)),
        ],
        out_specs=[
            pl.BlockSpec((tm, dh), lambda b, i: (b * nl + i, 0)),
            pl.BlockSpec((tm, dh), lambda b, i: (b * nl + i, 0)),
        ],
        out_shape=[
            jax.ShapeDtypeStruct((T, dh), F32),
            jax.ShapeDtypeStruct((T, dh), BF16),
        ],
        compiler_params=_cparams(("arbitrary", "arbitrary")),
        name="hyena_pre",
    )(uh, uh, uh, conv_w, conv_b.reshape(1, Ch))


def _dft_tables(L):
    N = 2 * L
    N1 = N // FFT_N2
    NB = L // FFT_N2
    r1 = jnp.arange(N1, dtype=jnp.int32)
    r2 = jnp.arange(FFT_N2, dtype=jnp.int32)

    def cos_sin(prod, period):
        ang = (prod % period).astype(F32) * (2.0 * math.pi / period)
        return jnp.cos(ang), jnp.sin(ang)

    c1, s1 = cos_sin(r1[:, None] * r1[None, :], N1)
    f1 = jnp.concatenate([c1, -s1], axis=0).astype(BF16)
    f1i = jnp.concatenate([c1[:NB], -s1[:NB]], axis=1).astype(BF16)
    ct, st = cos_sin(r1[:, None] * r2[None, :], N)
    c2, s2 = cos_sin(r2[:, None] * r2[None, :], FFT_N2)
    f2 = jnp.concatenate([c2, -s2], axis=1).astype(BF16)
    f2i = jnp.concatenate([jnp.concatenate([c2, s2], 1),
                           jnp.concatenate([-s2, c2], 1)], 0).astype(BF16)
    return dict(f1=f1, f1d=f1[:, :NB], f1i=f1i, tr=ct, ti=-st, f2=f2, f2i=f2i)


FFT_GROUP = 8


def _dft2_forward(xs, f1_ref, tr, ti, f2_ref):
    n1 = tr.shape[0]
    a = _dot(f1_ref[...], jnp.concatenate(xs, axis=1))
    rows = []
    for j in range(len(xs)):
        aj = a[:, j * FFT_N2:(j + 1) * FFT_N2]
        ar, ai = aj[:n1], aj[n1:]
        rows += [ar * tr - ai * ti, ar * ti + ai * tr]
    p = _dot(jnp.concatenate(rows, axis=0).astype(BF16), f2_ref[...])
    out = []
    for j in range(len(xs)):
        pr, pi = p[2 * j * n1:(2 * j + 1) * n1], p[(2 * j + 1) * n1:(2 * j + 2) * n1]
        out.append((pr[:, :FFT_N2] - pi[:, FFT_N2:], pr[:, FFT_N2:] + pi[:, :FFT_N2]))
    return out


def _filter_spec_kernel(kf_ref, f1_ref, tr_ref, ti_ref, f2_ref, h_ref, *, inv_n, unroll):
    tr, ti = tr_ref[...], ti_ref[...]

    def group(g, carry):
        c0 = g * FFT_GROUP
        xs = [kf_ref[c0 + j].astype(BF16) for j in range(FFT_GROUP)]
        for j, (sr, si) in enumerate(_dft2_forward(xs, f1_ref, tr, ti, f2_ref)):
            h_ref[c0 + j] = (jnp.concatenate([sr, si], axis=1) * inv_n).astype(BF16)
        return carry

    lax.fori_loop(0, kf_ref.shape[0] // FFT_GROUP, group, 0, unroll=unroll)


def _filter_spec(kf, tb, cc):
    Dh, n1, _ = kf.shape
    fix2 = lambda c: (0, 0)
    return pl.pallas_call(
        functools.partial(_filter_spec_kernel, inv_n=1.0 / (n1 * FFT_N2), unroll=2),
        grid=(Dh // cc,),
        in_specs=[
            pl.BlockSpec((cc, n1, FFT_N2), lambda c: (c, 0, 0)),
            pl.BlockSpec((2 * n1, n1), fix2),
            pl.BlockSpec((n1, FFT_N2), fix2),
            pl.BlockSpec((n1, FFT_N2), fix2),
            pl.BlockSpec((FFT_N2, 2 * FFT_N2), fix2),
        ],
        out_specs=pl.BlockSpec((cc, n1, 2 * FFT_N2), lambda c: (c, 0, 0)),
        out_shape=jax.ShapeDtypeStruct((Dh, n1, 2 * FFT_N2), BF16),
        compiler_params=_cparams(("arbitrary",)),
        name="filter_spec",
    )(kf, tb["f1"], tb["tr"], tb["ti"], tb["f2"])


def _fft_conv_kernel(x_ref, f1_ref, f1i_ref, tr_ref, ti_ref, f2_ref, f2i_ref, h_ref, o_ref, *, unroll):
    tr, ti = tr_ref[...], ti_ref[...]
    n1 = tr.shape[0]

    def group(g, carry):
        c0 = g * FFT_GROUP
        xs = [x_ref[c0 + j].astype(BF16) for j in range(FFT_GROUP)]
        ys = []
        for j, (sr, si) in enumerate(_dft2_forward(xs, f1_ref, tr, ti, f2_ref)):
            h = h_ref[c0 + j].astype(F32)
            hr, hi = h[:, :FFT_N2], h[:, FFT_N2:]
            ys.append(jnp.concatenate([sr * hr - si * hi, sr * hi + si * hr], axis=1))
        b = _dot(jnp.concatenate(ys, axis=0).astype(BF16), f2i_ref[...])
        cols = []
        for j in range(FFT_GROUP):
            bj = b[j * n1:(j + 1) * n1]
            br, bi = bj[:, :FFT_N2], bj[:, FFT_N2:]
            cols.append(jnp.concatenate([br * tr + bi * ti, bi * tr - br * ti], axis=0))
        y = _dot(f1i_ref[...], jnp.concatenate(cols, axis=1).astype(BF16))
        for j in range(FFT_GROUP):
            o_ref[c0 + j] = y[:, j * FFT_N2:(j + 1) * FFT_N2]
        return carry

    lax.fori_loop(0, x_ref.shape[0] // FFT_GROUP, group, 0, unroll=unroll)


def _fft_conv(x4, tb, h_spec, cc):
    B, Dh, nb, _ = x4.shape
    n1 = 2 * nb
    fix2 = lambda c, b: (0, 0)
    return pl.pallas_call(
        functools.partial(_fft_conv_kernel, unroll=2),
        grid=(Dh // cc, B),
        in_specs=[
            pl.BlockSpec((None, cc, nb, FFT_N2), lambda c, b: (b, c, 0, 0)),
            pl.BlockSpec((2 * n1, nb), fix2),
            pl.BlockSpec((nb, 2 * n1), fix2),
            pl.BlockSpec((n1, FFT_N2), fix2),
            pl.BlockSpec((n1, FFT_N2), fix2),
            pl.BlockSpec((FFT_N2, 2 * FFT_N2), fix2),
            pl.BlockSpec((2 * FFT_N2, 2 * FFT_N2), fix2),
            pl.BlockSpec((cc, n1, 2 * FFT_N2), lambda c, b: (c, 0, 0)),
        ],
        out_specs=pl.BlockSpec((None, cc, nb, FFT_N2), lambda c, b: (b, c, 0, 0)),
        out_shape=jax.ShapeDtypeStruct((B, Dh, nb, FFT_N2), F32),
        compiler_params=_cparams(("arbitrary", "arbitrary")),
        name="fft_conv",
    )(x4, tb["f1d"], tb["f1i"], tb["tr"], tb["ti"], tb["f2"], tb["f2i"], h_spec)


def _bias_tiles_kernel(table_ref, o_ref, *, tk, tq, n_buckets):
    h = pl.program_id(0)
    c = pl.program_id(1)
    nb = n_buckets // 2
    max_exact = nb // 2
    kv = lax.broadcasted_iota(jnp.int32, (tk, tq), 0)
    q = lax.broadcasted_iota(jnp.int32, (tk, tq), 1)
    rel = (c - 2) * tk + kv - q
    ret = jnp.where(rel > 0, nb, 0)
    n = jnp.abs(rel)
    large = max_exact + (jnp.log(jnp.maximum(n, 1).astype(F32) / max_exact)
                         / math.log(REL_MAX_DIST / max_exact) * (nb - max_exact)).astype(jnp.int32)
    large = jnp.minimum(large, nb - 1)
    bucket = ret + jnp.where(n < max_exact, n, large)
    out = jnp.zeros((tk, tq), F32)
    for b in range(n_buckets):
        out = jnp.where(bucket == b, table_ref[b, h], out)
    o_ref[...] = (out * LOG2E).astype(o_ref.dtype)


def _bias_tiles(rel_table, tk, tq):
    n_buckets, H = rel_table.shape
    n_cls = tq // tk + 4
    return pl.pallas_call(
        functools.partial(_bias_tiles_kernel, tk=tk, tq=tq, n_buckets=n_buckets),
        grid=(H, n_cls),
        in_specs=[pl.BlockSpec(memory_space=pltpu.SMEM)],
        out_specs=pl.BlockSpec((None, None, tk, tq), lambda h, c: (h, c, 0, 0)),
        out_shape=jax.ShapeDtypeStruct((H, n_cls, tk, tq), BF16),
        compiler_params=_cparams(("arbitrary", "arbitrary")),
        name="bias_tiles",
    )(rel_table)


def _attn_kernel(lam_ref, qt_ref, k_ref, vt_ref, bias_ref, g_ref, o_ref, acc1_ref, acc2_ref,
                 sa_ref, sb_ref, pa_ref, pb_ref, *, tk, tq, nk, hd, out_scale):
    qi = pl.program_id(2)
    dv = 2 * hd
    ratio = tq // tk
    qt = qt_ref[...]
    half = lax.broadcasted_iota(jnp.int32, qt.shape, 0) < hd
    zero = jnp.zeros_like(qt)
    q1 = jnp.where(half, qt, zero)
    q2 = jnp.where(half, zero, qt)
    acc1_ref[...] = jnp.zeros_like(acc1_ref)
    acc2_ref[...] = jnp.zeros_like(acc2_ref)

    def scores(ki, s_ref):
        off = pl.multiple_of(ki * tk, tk)
        kb = k_ref[pl.ds(off, tk), :]
        bias = bias_ref[jnp.clip(ki - ratio * qi, -2, ratio + 1) + 2]
        s1 = _dot(kb, q1).astype(BF16) + bias
        s_ref[0] = s1
        s2 = _dot(kb, q2).astype(BF16) + bias
        s_ref[1] = s2
        return (jnp.max(s1, axis=0, keepdims=True).astype(F32),
                jnp.max(s2, axis=0, keepdims=True).astype(F32))

    def probs(s_ref, p_ref, mt, m):
        m_out, alpha_out = [], []
        for j in range(2):
            m_new = jnp.maximum(m[j], mt[j])
            alpha_out.append(jnp.exp2(m[j] - m_new))
            p_ref[j] = jnp.exp2(s_ref[j] - m_new.astype(BF16))
            m_out.append(m_new)
        return tuple(m_out), tuple(alpha_out)

    def add_values(ki, p_ref, alpha):
        off = pl.multiple_of(ki * tk, tk)
        vb = vt_ref[:, pl.ds(off, tk)]
        for j, acc_ref in enumerate((acc1_ref, acc2_ref)):
            acc_ref[...] = alpha[j] * acc_ref[...] + _dot(vb, p_ref[j])

    def tick(i, s_new, s_cur, p_new, p_cur, mt_next, alpha, m, last_scores=True):
        mt_after = scores(i + 2, s_new) if last_scores else None
        m, alpha_next = probs(s_cur, p_new, mt_next, m)
        add_values(i, p_cur, alpha)
        return mt_after, alpha_next, m

    def pair(j, carry):
        mt_next, alpha, m = carry
        i = 2 * j
        mt_next, alpha, m = tick(i, sa_ref, sb_ref, pb_ref, pa_ref, mt_next, alpha, m)
        mt_next, alpha, m = tick(i + 1, sb_ref, sa_ref, pa_ref, pb_ref, mt_next, alpha, m)
        return mt_next, alpha, m

    neg = jnp.full((1, tq), -jnp.inf, F32)
    mt0 = scores(0, sa_ref)
    mt1 = scores(1, sb_ref)
    m, alpha = probs(sa_ref, pa_ref, mt0, (neg, neg))
    mt_last, alpha, m = lax.fori_loop(0, nk // 2 - 1, pair, (mt1, alpha, m))
    _, alpha_last, m = tick(nk - 2, None, sb_ref, pb_ref, pa_ref, mt_last, alpha, m, last_scores=False)
    add_values(nk - 1, pb_ref, alpha_last)
    a1 = acc1_ref[...]
    a2 = acc2_ref[...]
    o = a1[:dv] / a1[dv:dv + 1] - lam_ref[0] * (a2[:dv] / a2[dv:dv + 1])
    ms = jnp.mean(o * o, axis=0, keepdims=True)
    o = o * lax.rsqrt(ms + LN_EPS) * g_ref[...] * out_scale
    o_ref[...] = o.T.astype(BF16)


def _attention(lam, qt, k, vt, bias, g_col, B, L, H, hd, tk, tq, lam_init):
    Da = H * 2 * hd
    nq = L // tq
    nk = L // tk
    n_cls = bias.shape[1]
    assert nk % 2 == 0 and tq % tk == 0
    dvp = vt.shape[2]
    k3 = k.reshape(B, L, Da)
    out = pl.pallas_call(
        functools.partial(_attn_kernel, tk=tk, tq=tq, nk=nk, hd=hd, out_scale=1.0 - lam_init),
        grid=(B, H, nq),
        in_specs=[
            pl.BlockSpec(memory_space=pltpu.SMEM),
            pl.BlockSpec((None, 2 * hd, tq), lambda b, h, q: (b, h, q)),
            pl.BlockSpec((None, L, 2 * hd), lambda b, h, q: (b, 0, h)),
            pl.BlockSpec((None, None, dvp, L), lambda b, h, q: (b, h, 0, 0)),
            pl.BlockSpec((None, n_cls, tk, tq), lambda b, h, q: (h, 0, 0, 0)),
            pl.BlockSpec((2 * hd, 1), lambda b, h, q: (0, 0)),
        ],
        out_specs=pl.BlockSpec((None, tq, 2 * hd), lambda b, h, q: (b, q, h)),
        out_shape=jax.ShapeDtypeStruct((B, L, Da), BF16),
        scratch_shapes=[pltpu.VMEM((dvp, tq), F32), pltpu.VMEM((dvp, tq), F32),
                        pltpu.VMEM((2, tk, tq), BF16), pltpu.VMEM((2, tk, tq), BF16),
                        pltpu.VMEM((2, tk, tq), BF16), pltpu.VMEM((2, tk, tq), BF16)],
        compiler_params=_cparams(("arbitrary", "arbitrary", "arbitrary")),
        name="diff_attn",
    )(lam, qt, k3, vt, bias, g_col)
    return out.reshape(B * L, Da)


def _proj_out_kernel(cy_ref, vx_ref, x0_ref, hb_ref, ya_ref, x_ref, woh_ref, woa_ref, g_ref, b_ref,
                     wrc_ref, wrh_ref, x1_ref, eid_ref, gate_ref, *, alpha, n_groups, per_group):
    yh = ((cy_ref[...] + vx_ref[...] * hb_ref[...]) * x0_ref[...].astype(F32)).astype(BF16)
    mix = _dot(yh, woh_ref[...]) + _dot(ya_ref[...], woa_ref[...])
    x1 = _layer_norm_rows(alpha * x_ref[...] + mix, g_ref[...], b_ref[...])
    x1_ref[...] = x1
    hi = x1.astype(BF16)
    lo = (x1 - hi.astype(F32)).astype(BF16)
    lg2 = _dot(hi, wrc_ref[...])
    lg = lg2[:, :LANES] + lg2[:, LANES:] + _dot(lo, wrh_ref[...])
    col = lax.broadcasted_iota(jnp.int32, lg.shape, 1).astype(F32)
    far = jnp.float32(LANES)
    neg_inf = jnp.float32(-jnp.inf)

    def first_col(mask):
        return jnp.min(jnp.where(mask, col, far), axis=-1, keepdims=True)

    gmask = col < n_groups
    glog = jnp.where(gmask, lg, neg_inf)
    gmax = jnp.max(glog, axis=-1, keepdims=True)
    g_sel = first_col(glog == gmax)
    g_w = 1.0 / jnp.sum(jnp.where(gmask, jnp.exp(glog - gmax), 0.0), axis=-1, keepdims=True)
    lo_col = n_groups + g_sel * per_group
    emask = (col >= lo_col) & (col < lo_col + per_group)
    elog = jnp.where(emask, lg, neg_inf)
    emax = jnp.max(elog, axis=-1, keepdims=True)
    eexp = jnp.where(emask, jnp.exp(elog - emax), 0.0)
    prob = eexp / jnp.sum(eexp, axis=-1, keepdims=True)
    p1 = jnp.max(prob, axis=-1, keepdims=True)
    i1 = first_col(emask & (prob == p1))
    mask2 = emask & (col != i1)
    p2 = jnp.max(jnp.where(mask2, prob, -1.0), axis=-1, keepdims=True)
    i2 = first_col(mask2 & (prob == p2))
    psum = p1 + p2
    eid = jnp.where(col == 0, i1 - n_groups, jnp.where(col == 1, i2 - n_groups, 0.0))
    eid_ref[...] = eid.astype(jnp.int32)
    gate_ref[...] = jnp.where(col == 0, g_w * p1 / psum, jnp.where(col == 1, g_w * p2 / psum, 0.0))


def _proj_out(cy, vx, x0c, hb, ya, x, woh, woa, g, b, wrc, wrh, alpha, n_groups, per_group, tm):
    T, D = x.shape
    Dh = cy.shape[1]
    Da = ya.shape[1]
    row = lambda i: (i, 0)
    fix = lambda i: (0, 0)
    return pl.pallas_call(
        functools.partial(_proj_out_kernel, alpha=alpha, n_groups=n_groups, per_group=per_group),
        grid=(T // tm,),
        in_specs=[
            pl.BlockSpec((tm, Dh), row),
            pl.BlockSpec((tm, Dh), row),
            pl.BlockSpec((tm, Dh), row),
            pl.BlockSpec((1, Dh), fix),
            pl.BlockSpec((tm, Da), row),
            pl.BlockSpec((tm, D), row),
            pl.BlockSpec((Dh, D), fix),
            pl.BlockSpec((Da, D), fix),
            pl.BlockSpec((1, D), fix),
            pl.BlockSpec((1, D), fix),
            pl.BlockSpec((D, 2 * LANES), fix),
            pl.BlockSpec((D, LANES), fix),
        ],
        out_specs=[
            pl.BlockSpec((tm, D), row),
            pl.BlockSpec((tm, LANES), row),
            pl.BlockSpec((tm, LANES), row),
        ],
        out_shape=[
            jax.ShapeDtypeStruct((T, D), F32),
            jax.ShapeDtypeStruct((T, LANES), jnp.int32),
            jax.ShapeDtypeStruct((T, LANES), F32),
        ],
        compiler_params=_cparams(("arbitrary",)),
        name="proj_out_ln_router",
    )(cy, vx, x0c, hb, ya, x, woh, woa, g, b, wrc, wrh)


def _moe_kernel(be_ref, nu_ref, src_ref, srcn_ref, dst_ref, x_hbm, wg_ref, wu_ref, wd_ref, out_hbm,
                xbuf, ybuf, wgb, wub, wdb, sem_in, sem_out, *, bm, chunk):
    i = pl.program_id(0)
    n_used = nu_ref[0]
    slot = i % 2

    def gather_row(idx_ref, s, r):
        return pltpu.make_async_copy(x_hbm.at[pl.ds(idx_ref[0, r], 1), :],
                                     xbuf.at[s, pl.ds(r, 1), :], sem_in.at[s])

    def scatter_row(s, r):
        return pltpu.make_async_copy(ybuf.at[s, pl.ds(r, 1), :],
                                     out_hbm.at[pl.ds(dst_ref[0, r], 1), :], sem_out.at[s])

    def wait_gather(s):
        pltpu.make_async_copy(x_hbm.at[pl.ds(0, bm), :], xbuf.at[s], sem_in.at[s]).wait()

    def wait_scatter(s):
        pltpu.make_async_copy(ybuf.at[s], out_hbm.at[pl.ds(0, bm), :], sem_out.at[s]).wait()

    @pl.when(i == 0)
    def _():
        def body(r, c):
            gather_row(src_ref, 0, r).start()
            return c

        lax.fori_loop(0, bm, body, 0, unroll=8)

    def run(cur, nxt):
        changed = jnp.logical_or(i == 0, be_ref[i] != be_ref[jnp.maximum(i - 1, 0)])

        @pl.when(changed)
        def _():
            wgb[...] = wg_ref[...].astype(BF16)
            wub[...] = wu_ref[...].astype(BF16)
            wdb[...] = wd_ref[...].astype(BF16)

        wait_gather(cur)
        for r in range(bm):
            gather_row(srcn_ref, nxt, r).start()
        xb = xbuf[cur].astype(BF16)
        hg = _dot(xb, wgb[...])
        hu = _dot(xb, wub[...])
        act = ((hg / (1.0 + jnp.exp(-hg))) * hu).astype(BF16)
        for c in range(bm // chunk):
            rows = slice(c * chunk, (c + 1) * chunk)
            ybuf[cur, rows, :] = _dot(act[rows], wdb[...])
            for r in range(c * chunk, (c + 1) * chunk):
                scatter_row(cur, r).start()

        @pl.when(i > 0)
        def _():
            wait_scatter(nxt)

        @pl.when(i == n_used - 1)
        def _():
            wait_scatter(cur)
            wait_gather(nxt)

    for parity in (0, 1):
        @pl.when(jnp.logical_and(i < n_used, slot == parity))
        def _():
            run(parity, 1 - parity)


def _moe_experts(block_e, n_used, src, dst, x1, w_gate, w_up, w_down, n_rows_out, bm):
    n_blocks = block_e.shape[0]
    T, D = x1.shape
    E, _, De = w_gate.shape
    idx_block = (None, 1, bm)
    grid_spec = pltpu.PrefetchScalarGridSpec(
        num_scalar_prefetch=2,
        grid=(n_blocks,),
        in_specs=[
            pl.BlockSpec(idx_block, lambda i, be, nu: (i, 0, 0), memory_space=pltpu.SMEM),
            pl.BlockSpec(idx_block, lambda i, be, nu: (jnp.minimum(i + 1, n_blocks - 1), 0, 0),
                         memory_space=pltpu.SMEM),
            pl.BlockSpec(idx_block, lambda i, be, nu: (i, 0, 0), memory_space=pltpu.SMEM),
            pl.BlockSpec(memory_space=pl.ANY),
            pl.BlockSpec((None, D, De), lambda i, be, nu: (be[i], 0, 0)),
            pl.BlockSpec((None, D, De), lambda i, be, nu: (be[i], 0, 0)),
            pl.BlockSpec((None, De, D), lambda i, be, nu: (be[i], 0, 0)),
        ],
        out_specs=pl.BlockSpec(memory_space=pl.ANY),
        scratch_shapes=[
            pltpu.VMEM((2, bm, D), F32),
            pltpu.VMEM((2, bm, D), F32),
            pltpu.VMEM((D, De), BF16),
            pltpu.VMEM((D, De), BF16),
            pltpu.VMEM((De, D), BF16),
            pltpu.SemaphoreType.DMA((2,)),
            pltpu.SemaphoreType.DMA((2,)),
        ],
    )
    return pl.pallas_call(
        functools.partial(_moe_kernel, bm=bm, chunk=min(bm, 128)),
        grid_spec=grid_spec,
        out_shape=jax.ShapeDtypeStruct((n_rows_out, D), F32),
        compiler_params=_cparams(("arbitrary",)),
        name="moe_experts",
    )(block_e, n_used, src, src, dst, x1, w_gate, w_up, w_down)


def _combine_kernel(ya_ref, yb_ref, gate_ref, x1_ref, g_ref, b_ref, *o_refs, alpha, na):
    gates = gate_ref[...]
    y = gates[:, 0:1] * ya_ref[...] + gates[:, 1:2] * yb_ref[...]
    res = _layer_norm_rows(alpha * x1_ref[...] + y, g_ref[...], b_ref[...])
    if len(o_refs) == 1:
        o_refs[0][...] = res
    else:
        i = pl.program_id(0)

        @pl.when(i < na)
        def _():
            o_refs[0][...] = res

        @pl.when(i >= na)
        def _():
            o_refs[1][...] = res


def _combine_ln(y2, gates, x1, g, b, alpha, tm, split_rows=None):
    T, D = x1.shape
    nt = T // tm
    row = lambda i: (i, 0)
    fix = lambda i: (0, 0)
    if split_rows is None:
        na = nt
        out_specs = pl.BlockSpec((tm, D), row)
        out_shape = jax.ShapeDtypeStruct((T, D), F32)
    else:
        na = split_rows // tm
        out_specs = [pl.BlockSpec((tm, D), lambda i: (jnp.minimum(i, na - 1), 0)),
                     pl.BlockSpec((tm, D), lambda i: (jnp.maximum(i - na, 0), 0))]
        out_shape = [jax.ShapeDtypeStruct((split_rows, D), F32),
                     jax.ShapeDtypeStruct((T - split_rows, D), F32)]
    return pl.pallas_call(
        functools.partial(_combine_kernel, alpha=alpha, na=na),
        grid=(nt,),
        in_specs=[
            pl.BlockSpec((tm, D), row),
            pl.BlockSpec((tm, D), lambda i: (nt + i, 0)),
            pl.BlockSpec((tm, LANES), row),
            pl.BlockSpec((tm, D), row),
            pl.BlockSpec((1, D), fix),
            pl.BlockSpec((1, D), fix),
        ],
        out_specs=out_specs,
        out_shape=out_shape,
        compiler_params=_cparams(("arbitrary",)),
        name="moe_combine_ln",
    )(y2, y2, gates, x1, g, b)


def _hyena_filter(L, dh, f_w1, f_b1, f_freq, f_w2, f_b2, f_w3):
    emb = f_w1.shape[0]
    bands = (emb - 1) // 2
    t = jnp.linspace(0.0, 1.0, L, dtype=F32)[:, None]
    w = (2.0 * math.pi / L) * jnp.arange(L, dtype=F32)[:, None]
    f = jnp.linspace(1e-4, bands - 1, bands, dtype=F32)[None, :]
    z = jnp.concatenate([t, jnp.cos(f * w), -jnp.sin(f * w)], axis=-1)
    hp = lax.Precision.HIGHEST
    max_decay = math.log(DECAY_TARGET) / FAST_DECAY_PCT
    min_decay = math.log(DECAY_TARGET) / SLOW_DECAY_PCT
    deltas = jnp.abs(jnp.linspace(min_decay, max_decay, dh, dtype=F32))

    def taps(zz, tt, w3):
        h = jnp.sin(f_freq * (jnp.dot(zz, f_w1, precision=hp) + f_b1))
        h = jnp.sin(f_freq * (jnp.dot(h, f_w2, precision=hp) + f_b2))
        return jnp.dot(h, w3, precision=hp) * jnp.exp(-tt * deltas)

    h_fwd = taps(z, t, f_w3[:, :dh])
    h_bwd_rev = taps(z[::-1], t[::-1], f_w3[:, dh:])[:L - 1]
    k_full = jnp.concatenate([h_fwd, jnp.zeros((1, dh), F32), h_bwd_rev], axis=0)
    return k_full / jnp.sum(jnp.abs(k_full), axis=0, keepdims=True)


def _dispatch_tables(eid, n_experts, bm):
    T = eid.shape[0]
    A = T * TOP_K
    flat_e = eid.reshape(A)
    se, order = lax.sort((flat_e, jnp.arange(A, dtype=jnp.int32)), num_keys=1, is_stable=True)
    experts = jnp.arange(n_experts, dtype=jnp.int32)
    onehot = se[:, None] == experts[None, :]
    counts = jnp.sum(onehot, axis=0, dtype=jnp.int32)
    starts = jnp.cumsum(counts) - counts
    pcounts = (counts + bm - 1) // bm * bm
    pends = jnp.cumsum(pcounts)
    pstarts = pends - pcounts
    n_blocks = -(-A // bm) + n_experts
    P = n_blocks * bm
    first_row = jnp.arange(n_blocks, dtype=jnp.int32) * bm
    block_e = jnp.minimum(jnp.sum(pends[None, :] <= first_row[:, None], axis=1, dtype=jnp.int32),
                          n_experts - 1)
    shift = jnp.sum(jnp.where(onehot, (pstarts - starts)[None, :], 0), axis=1)
    dest = jnp.arange(A, dtype=jnp.int32) + shift
    row_asg = jnp.full((P,), -1, jnp.int32).at[dest].set(order, unique_indices=True,
                                                         indices_are_sorted=True)
    pad = row_asg < 0
    src = jnp.maximum(row_asg, 0) // TOP_K
    dst = jnp.where(pad, A + jnp.cumsum(pad.astype(jnp.int32)) - 1, (row_asg % TOP_K) * T + row_asg // TOP_K)
    n_used = (pends[-1] // bm).astype(jnp.int32).reshape(1)
    return block_e, n_used, src.reshape(n_blocks, 1, bm), dst.astype(jnp.int32).reshape(n_blocks, 1, bm), P


def _pick(n, pref):
    t = min(n, pref)
    assert n % t == 0, (n, t)
    return t


def _trunk(xa, xb, p):
    Ba, L, D = xa.shape
    Bb = xb.shape[0]
    assert xb.shape[1:] == (L, D)
    B = Ba + Bb
    T = B * L
    depth = p["w_in"].shape[0]
    dh = p["h_bias"].shape[1]
    H = p["rel_table"].shape[1]
    hd = p["lam_q1"].shape[1]
    da = H * 2 * hd
    n_groups = p["w_route_group"].shape[2]
    n_experts = p["w_route_expert"].shape[2]
    per_group = n_experts // n_groups
    alpha = (2.0 * depth) ** 0.25
    assert L % FFT_N2 == 0 and (L // FFT_N2) % 8 == 0
    assert 2 * hd == LANES and dh % LANES == 0 and n_groups + n_experts <= LANES

    tm = _pick(L, 512)
    tk_attn = _pick(L, 512)
    tq_attn = _pick(L, 1024)
    bm = 512

    xs = _ln_embed(xa.reshape(Ba * L, D), xb.reshape(Bb * L, D), p["ln_emb_g"], p["ln_emb_b"], tm)
    tables = _dft_tables(L)
    nb = L // FFT_N2
    bias = _bias_tiles(p["rel_table"], tk_attn, tq_attn)

    for l in range(depth):
        lam_init = 0.8 - 0.6 * math.exp(-0.3 * l)
        w_in = p["w_in"][l]
        wh = w_in[:, :3 * dh].astype(BF16)
        wa = w_in[:, 3 * dh:]
        wqt = (wa[:, :da] * (hd ** -0.5 * LOG2E)).T.astype(BF16)
        wk = wa[:, da:2 * da].astype(BF16)
        wvt = wa[:, 2 * da:].T.astype(BF16)
        uh, k, qt, vt = _proj_in(xs, wh, wk, wqt, wvt, B, L, H, tm)

        vx, x0c = _hyena_pre(uh, p["conv_w"][l], p["conv_b"][l], B, L, tm)
        k_full = _hyena_filter(L, dh, p["f_w1"][l], p["f_b1"][l], p["f_freq"][l], p["f_w2"][l],
                               p["f_b2"][l], p["f_w3"][l])
        h_spec = _filter_spec(k_full.T.reshape(dh, 2 * nb, FFT_N2), tables, LANES)
        vx4 = vx.reshape(B, L, dh).transpose(0, 2, 1).reshape(B, dh, nb, FFT_N2)
        cy = _fft_conv(vx4, tables, h_spec, LANES)
        cy = cy.reshape(B, dh, L).transpose(0, 2, 1).reshape(T, dh)

        lam = (jnp.exp(jnp.sum(p["lam_q1"][l] * p["lam_k1"][l]))
               - jnp.exp(jnp.sum(p["lam_q2"][l] * p["lam_k2"][l])) + lam_init).reshape(1).astype(F32)
        ya = _attention(lam, qt, k, vt, bias, p["subln_g"][l].reshape(2 * hd, 1), B, L, H, hd, tk_attn,
                        tq_attn, lam_init)

        w_out = p["w_out"][l]
        w_r = jnp.concatenate([p["w_route_group"][l], p["w_route_expert"][l]], axis=1)
        w_r = jnp.pad(w_r, ((0, 0), (0, LANES - w_r.shape[1])))
        w_r_hi = w_r.astype(BF16)
        w_r_lo = (w_r - w_r_hi.astype(F32)).astype(BF16)
        x1, eid, gates = _proj_out(
            cy, vx, x0c, p["h_bias"][l].reshape(1, dh), ya, xs,
            w_out[:dh].astype(BF16), w_out[dh:].astype(BF16),
            p["ln1_g"][l].reshape(1, D), p["ln1_b"][l].reshape(1, D),
            jnp.concatenate([w_r_hi, w_r_lo], axis=1), w_r_hi,
            alpha, n_groups, per_group, tm)

        block_e, n_used, src, dst, n_rows = _dispatch_tables(eid[:, :TOP_K], n_experts, bm)
        y2 = _moe_experts(block_e, n_used, src, dst, x1, p["w_gate"][l], p["w_up"][l], p["w_down"][l],
                          n_rows, bm)
        xs = _combine_ln(y2, gates, x1, p["ln2_g"][l].reshape(1, D), p["ln2_b"][l].reshape(1, D),
                         alpha, tm, split_rows=Ba * L if l == depth - 1 else None)
    return xs[0].reshape(Ba, L, D), xs[1].reshape(Bb, L, D)


def kernel(x_prompt, x_sample, ln_emb_g, ln_emb_b, rel_table, w_in, conv_w, conv_b, f_w1, f_b1, f_freq,
           f_w2, f_b2, f_w3, h_bias, lam_q1, lam_k1, lam_q2, lam_k2, subln_g, w_out, ln1_g, ln1_b,
           w_route_group, w_route_expert, w_gate, w_up, w_down, ln2_g, ln2_b):
    p = dict(ln_emb_g=ln_emb_g, ln_emb_b=ln_emb_b, rel_table=rel_table, w_in=w_in, conv_w=conv_w,
             conv_b=conv_b, f_w1=f_w1, f_b1=f_b1, f_freq=f_freq, f_w2=f_w2, f_b2=f_b2, f_w3=f_w3,
             h_bias=h_bias, lam_q1=lam_q1, lam_k1=lam_k1, lam_q2=lam_q2, lam_k2=lam_k2,
             subln_g=subln_g, w_out=w_out, ln1_g=ln1_g, ln1_b=ln1_b, w_route_group=w_route_group,
             w_route_expert=w_route_expert, w_gate=w_gate, w_up=w_up, w_down=w_down, ln2_g=ln2_g,
             ln2_b=ln2_b)
    return _trunk(x_prompt, x_sample, p)
```

```python
import functools
import math

import jax
import jax.numpy as jnp
from jax import lax
from jax.experimental import pallas as pl
from jax.experimental.pallas import tpu as pltpu

F32 = jnp.float32
BF16 = jnp.bfloat16

LN_EPS = 1e-5
REL_MAX_DIST = 128
TOP_K = 2
DECAY_TARGET = 1e-2
FAST_DECAY_PCT = 0.3
SLOW_DECAY_PCT = 1.5

LANES = 128
BF16_SUBLANES = 16
LOG2E = 1.4426950408889634
FFT_N2 = 128
VMEM_LIMIT_BYTES = 56 * 1024 * 1024


def _cparams(sem, flags=None):
    return pltpu.CompilerParams(dimension_semantics=sem, vmem_limit_bytes=VMEM_LIMIT_BYTES, flags=flags)


def _dot(a, b):
    return jnp.dot(a, b, preferred_element_type=F32)


def _layer_norm_rows(z, g, b):
    mu = jnp.mean(z, axis=-1, keepdims=True)
    zc = z - mu
    var = jnp.mean(zc * zc, axis=-1, keepdims=True)
    return zc * lax.rsqrt(var + LN_EPS) * g + b


def _ln_kernel(xa_ref, xb_ref, g_ref, b_ref, o_ref, *, na):
    i = pl.program_id(0)

    @pl.when(i < na)
    def _():
        o_ref[...] = _layer_norm_rows(xa_ref[...], g_ref[...], b_ref[...])

    @pl.when(i >= na)
    def _():
        o_ref[...] = _layer_norm_rows(xb_ref[...], g_ref[...], b_ref[...])


def _ln_embed(xa, xb, g, b, tm):
    Ta, D = xa.shape
    Tb = xb.shape[0]
    na = Ta // tm
    nb = Tb // tm
    return pl.pallas_call(
        functools.partial(_ln_kernel, na=na),
        grid=(na + nb,),
        in_specs=[
            pl.BlockSpec((tm, D), lambda i: (jnp.minimum(i, na - 1), 0)),
            pl.BlockSpec((tm, D), lambda i: (jnp.maximum(i - na, 0), 0)),
            pl.BlockSpec((1, D), lambda i: (0, 0)),
            pl.BlockSpec((1, D), lambda i: (0, 0)),
        ],
        out_specs=pl.BlockSpec((tm, D), lambda i: (i, 0)),
        out_shape=jax.ShapeDtypeStruct((Ta + Tb, D), F32),
        compiler_params=_cparams(("arbitrary",)),
        name="ln_embed",
    )(xa, xb, g.reshape(1, D), b.reshape(1, D))


def _proj_in_kernel(x_ref, wh_ref, wk_ref, wqt_ref, wvt_ref, uh_ref, k_ref, qt_ref, vt_ref):
    xb = x_ref[...].astype(BF16)
    uh_ref[...] = _dot(xb, wh_ref[...]).astype(BF16)
    k_ref[...] = _dot(xb, wk_ref[...]).astype(BF16)
    nt = (((1,), (1,)), ((), ()))
    qt_ref[...] = lax.dot_general(wqt_ref[...], xb, nt, preferred_element_type=F32).astype(BF16)
    vt = lax.dot_general(wvt_ref[...], xb, nt, preferred_element_type=F32).astype(BF16)
    n_heads, dvp, tm = vt_ref.shape
    dv = vt.shape[0] // n_heads
    pad_rows = lax.broadcasted_iota(jnp.int32, (dvp - dv, tm), 0)
    ones_pad = jnp.where(pad_rows == 0, 1.0, 0.0).astype(BF16)
    for h in range(n_heads):
        vt_ref[h, 0:dv, :] = vt[h * dv:(h + 1) * dv]
        vt_ref[h, dv:dvp, :] = ones_pad


def _proj_in(x2d, wh, wk, wqt, wvt, B, L, H, tm):
    T, D = x2d.shape
    Ch = wh.shape[1]
    Da = wk.shape[1]
    nl = L // tm
    dvp = Da // H + BF16_SUBLANES
    return pl.pallas_call(
        _proj_in_kernel,
        grid=(B, nl),
        in_specs=[
            pl.BlockSpec((tm, D), lambda b, i: (b * nl + i, 0)),
            pl.BlockSpec((D, Ch), lambda b, i: (0, 0)),
            pl.BlockSpec((D, Da), lambda b, i: (0, 0)),
            pl.BlockSpec((Da, D), lambda b, i: (0, 0)),
            pl.BlockSpec((Da, D), lambda b, i: (0, 0)),
        ],
        out_specs=[
            pl.BlockSpec((tm, Ch), lambda b, i: (b * nl + i, 0)),
            pl.BlockSpec((tm, Da), lambda b, i: (b * nl + i, 0)),
            pl.BlockSpec((None, Da, tm), lambda b, i: (b, 0, i)),
            pl.BlockSpec((None, H, dvp, tm), lambda b, i: (b, 0, 0, i)),
        ],
        out_shape=[
            jax.ShapeDtypeStruct((T, Ch), BF16),
            jax.ShapeDtypeStruct((T, Da), BF16),
            jax.ShapeDtypeStruct((B, Da, L), BF16),
            jax.ShapeDtypeStruct((B, H, dvp, L), BF16),
        ],
        compiler_params=_cparams(("arbitrary", "arbitrary")),
        name="proj_in",
    )(x2d, wh, wk, wqt, wvt)


def _hyena_pre_kernel(u_ref, up_ref, un_ref, w_ref, b_ref, vx_ref, x0_ref, *, dh, halo):
    i = pl.program_id(1)
    last = pl.num_programs(1) - 1
    u = u_ref[...].astype(F32)
    tm = u.shape[0]
    prev_row = up_ref[halo - 1:halo, :].astype(F32)
    next_row = un_ref[0:1, :].astype(F32)
    prev_row = jnp.where(i == 0, 0.0, prev_row)
    next_row = jnp.where(i == last, 0.0, next_row)
    rows = lax.broadcasted_iota(jnp.int32, (tm, 1), 0)
    um1 = jnp.where(rows == 0, prev_row, pltpu.roll(u, 1, axis=0))
    up1 = jnp.where(rows == tm - 1, next_row, pltpu.roll(u, tm - 1, axis=0))
    w = w_ref[...]
    y = b_ref[...] + um1 * w[0:1, :]
    y = y + u * w[1:2, :]
    y = y + up1 * w[2:3, :]
    x0_ref[...] = y[:, :dh].astype(BF16)
    vx_ref[...] = y[:, 2 * dh:] * y[:, dh:2 * dh]


def _hyena_pre(uh, conv_w, conv_b, B, L, tm):
    T, Ch = uh.shape
    dh = Ch // 3
    halo = 16
    nl = L // tm
    r = tm // halo
    nhalo = T // halo
    return pl.pallas_call(
        functools.partial(_hyena_pre_kernel, dh=dh, halo=halo),
        grid=(B, nl),
        in_specs=[
            pl.BlockSpec((tm, Ch), lambda b, i: (b * nl + i, 0)),
            pl.BlockSpec((halo, Ch), lambda b, i: (jnp.maximum((b * nl + i) * r - 1, 0), 0)),
            pl.BlockSpec((halo, Ch), lambda b, i: (jnp.minimum((b * nl + i + 1) * r, nhalo - 1), 0)),
            pl.BlockSpec((3, Ch), lambda b, i: (0, 0)),
            pl.BlockSpec((1, Ch), lambda b, i: (0, 0)),
        ],
        out_specs=[
            pl.BlockSpec((tm, dh), lambda b, i: (b * nl + i, 0)),
            pl.BlockSpec((tm, dh), lambda b, i: (b * nl + i, 0)),
        ],
        out_shape=[
            jax.ShapeDtypeStruct((T, dh), F32),
            jax.ShapeDtypeStruct((T, dh), BF16),
        ],
        compiler_params=_cparams(("arbitrary", "arbitrary")),
        name="hyena_pre",
    )(uh, uh, uh, conv_w, conv_b.reshape(1, Ch))


def _dft_tables(L):
    N = 2 * L
    N1 = N // FFT_N2
    NB = L // FFT_N2
    r1 = jnp.arange(N1, dtype=jnp.int32)
    r2 = jnp.arange(FFT_N2, dtype=jnp.int32)

    def cos_sin(prod, period):
        ang = (prod % period).astype(F32) * (2.0 * math.pi / period)
        return jnp.cos(ang), jnp.sin(ang)

    c1, s1 = cos_sin(r1[:, None] * r1[None, :], N1)
    f1 = jnp.concatenate([c1, -s1], axis=0).astype(BF16)
    f1i = jnp.concatenate([c1[:NB], -s1[:NB]], axis=1).astype(BF16)
    ct, st = cos_sin(r1[:, None] * r2[None, :], N)
    c2, s2 = cos_sin(r2[:, None] * r2[None, :], FFT_N2)
    f2 = jnp.concatenate([c2, -s2], axis=1).astype(BF16)
    f2i = jnp.concatenate([jnp.concatenate([c2, s2], 1),
                           jnp.concatenate([-s2, c2], 1)], 0).astype(BF16)
    return dict(f1=f1, f1d=f1[:, :NB], f1i=f1i, tr=ct, ti=-st, f2=f2, f2i=f2i)


FFT_GROUP = 8


def _dft2_forward(xs, f1_ref, tr, ti, f2_ref):
    n1 = tr.shape[0]
    a = _dot(f1_ref[...], jnp.concatenate(xs, axis=1))
    rows = []
    for j in range(len(xs)):
        aj = a[:, j * FFT_N2:(j + 1) * FFT_N2]
        ar, ai = aj[:n1], aj[n1:]
        rows += [ar * tr - ai * ti, ar * ti + ai * tr]
    p = _dot(jnp.concatenate(rows, axis=0).astype(BF16), f2_ref[...])
    out = []
    for j in range(len(xs)):
        pr, pi = p[2 * j * n1:(2 * j + 1) * n1], p[(2 * j + 1) * n1:(2 * j + 2) * n1]
        out.append((pr[:, :FFT_N2] - pi[:, FFT_N2:], pr[:, FFT_N2:] + pi[:, :FFT_N2]))
    return out


def _filter_spec_kernel(kf_ref, f1_ref, tr_ref, ti_ref, f2_ref, h_ref, *, inv_n, unroll):
    tr, ti = tr_ref[...], ti_ref[...]

    def group(g, carry):
        c0 = g * FFT_GROUP
        xs = [kf_ref[c0 + j].astype(BF16) for j in range(FFT_GROUP)]
        for j, (sr, si) in enumerate(_dft2_forward(xs, f1_ref, tr, ti, f2_ref)):
            h_ref[c0 + j] = (jnp.concatenate([sr, si], axis=1) * inv_n).astype(BF16)
        return carry

    lax.fori_loop(0, kf_ref.shape[0] // FFT_GROUP, group, 0, unroll=unroll)


def _filter_spec(kf, tb, cc):
    Dh, n1, _ = kf.shape
    fix2 = lambda c: (0, 0)
    return pl.pallas_call(
        functools.partial(_filter_spec_kernel, inv_n=1.0 / (n1 * FFT_N2), unroll=2),
        grid=(Dh // cc,),
        in_specs=[
            pl.BlockSpec((cc, n1, FFT_N2), lambda c: (c, 0, 0)),
            pl.BlockSpec((2 * n1, n1), fix2),
            pl.BlockSpec((n1, FFT_N2), fix2),
            pl.BlockSpec((n1, FFT_N2), fix2),
            pl.BlockSpec((FFT_N2, 2 * FFT_N2), fix2),
        ],
        out_specs=pl.BlockSpec((cc, n1, 2 * FFT_N2), lambda c: (c, 0, 0)),
        out_shape=jax.ShapeDtypeStruct((Dh, n1, 2 * FFT_N2), BF16),
        compiler_params=_cparams(("arbitrary",)),
        name="filter_spec",
    )(kf, tb["f1"], tb["tr"], tb["ti"], tb["f2"])


def _fft_conv_kernel(x_ref, f1_ref, f1i_ref, tr_ref, ti_ref, f2_ref, f2i_ref, h_ref, o_ref, *, unroll):
    tr, ti = tr_ref[...], ti_ref[...]
    n1 = tr.shape[0]

    def group(g, carry):
        c0 = g * FFT_GROUP
        xs = [x_ref[c0 + j].astype(BF16) for j in range(FFT_GROUP)]
        ys = []
        for j, (sr, si) in enumerate(_dft2_forward(xs, f1_ref, tr, ti, f2_ref)):
            h = h_ref[c0 + j].astype(F32)
            hr, hi = h[:, :FFT_N2], h[:, FFT_N2:]
            ys.append(jnp.concatenate([sr * hr - si * hi, sr * hi + si * hr], axis=1))
        b = _dot(jnp.concatenate(ys, axis=0).astype(BF16), f2i_ref[...])
        cols = []
        for j in range(FFT_GROUP):
            bj = b[j * n1:(j + 1) * n1]
            br, bi = bj[:, :FFT_N2], bj[:, FFT_N2:]
            cols.append(jnp.concatenate([br * tr + bi * ti, bi * tr - br * ti], axis=0))
        y = _dot(f1i_ref[...], jnp.concatenate(cols, axis=1).astype(BF16))
        for j in range(FFT_GROUP):
            o_ref[c0 + j] = y[:, j * FFT_N2:(j + 1) * FFT_N2]
        return carry

    lax.fori_loop(0, x_ref.shape[0] // FFT_GROUP, group, 0, unroll=unroll)


def _fft_conv(x4, tb, h_spec, cc):
    B, Dh, nb, _ = x4.shape
    n1 = 2 * nb
    fix2 = lambda c, b: (0, 0)
    return pl.pallas_call(
        functools.partial(_fft_conv_kernel, unroll=2),
        grid=(Dh // cc, B),
        in_specs=[
            pl.BlockSpec((None, cc, nb, FFT_N2), lambda c, b: (b, c, 0, 0)),
            pl.BlockSpec((2 * n1, nb), fix2),
            pl.BlockSpec((nb, 2 * n1), fix2),
            pl.BlockSpec((n1, FFT_N2), fix2),
            pl.BlockSpec((n1, FFT_N2), fix2),
            pl.BlockSpec((FFT_N2, 2 * FFT_N2), fix2),
            pl.BlockSpec((2 * FFT_N2, 2 * FFT_N2), fix2),
            pl.BlockSpec((cc, n1, 2 * FFT_N2), lambda c, b: (c, 0, 0)),
        ],
        out_specs=pl.BlockSpec((None, cc, nb, FFT_N2), lambda c, b: (b, c, 0, 0)),
        out_shape=jax.ShapeDtypeStruct((B, Dh, nb, FFT_N2), F32),
        compiler_params=_cparams(("arbitrary", "arbitrary")),
        name="fft_conv",
    )(x4, tb["f1d"], tb["f1i"], tb["tr"], tb["ti"], tb["f2"], tb["f2i"], h_spec)


def _bias_tiles_kernel(table_ref, o_ref, *, tk, tq, n_buckets):
    h = pl.program_id(0)
    c = pl.program_id(1)
    nb = n_buckets // 2
    max_exact = nb // 2
    kv = lax.broadcasted_iota(jnp.int32, (tk, tq), 0)
    q = lax.broadcasted_iota(jnp.int32, (tk, tq), 1)
    rel = (c - 2) * tk + kv - q
    ret = jnp.where(rel > 0, nb, 0)
    n = jnp.abs(rel)
    large = max_exact + (jnp.log(jnp.maximum(n, 1).astype(F32) / max_exact)
                         / math.log(REL_MAX_DIST / max_exact) * (nb - max_exact)).astype(jnp.int32)
    large = jnp.minimum(large, nb - 1)
    bucket = ret + jnp.where(n < max_exact, n, large)
    out = jnp.zeros((tk, tq), F32)
    for b in range(n_buckets):
        out = jnp.where(bucket == b, table_ref[b, h], out)
    o_ref[...] = (out * LOG2E).astype(o_ref.dtype)


def _bias_tiles(rel_table, tk, tq):
    n_buckets, H = rel_table.shape
    n_cls = tq // tk + 4
    return pl.pallas_call(
        functools.partial(_bias_tiles_kernel, tk=tk, tq=tq, n_buckets=n_buckets),
        grid=(H, n_cls),
        in_specs=[pl.BlockSpec(memory_space=pltpu.SMEM)],
        out_specs=pl.BlockSpec((None, None, tk, tq), lambda h, c: (h, c, 0, 0)),
        out_shape=jax.ShapeDtypeStruct((H, n_cls, tk, tq), BF16),
        compiler_params=_cparams(("arbitrary", "arbitrary")),
        name="bias_tiles",
    )(rel_table)


def _attn_kernel(lam_ref, qt_ref, k_ref, vt_ref, bias_ref, g_ref, o_ref, acc1_ref, acc2_ref,
                 sa_ref, sb_ref, pa_ref, pb_ref, *, tk, tq, nk, hd, out_scale):
    qi = pl.program_id(2)
    dv = 2 * hd
    ratio = tq // tk
    qt = qt_ref[...]
    half = lax.broadcasted_iota(jnp.int32, qt.shape, 0) < hd
    zero = jnp.zeros_like(qt)
    q1 = jnp.where(half, qt, zero)
    q2 = jnp.where(half, zero, qt)
    acc1_ref[...] = jnp.zeros_like(acc1_ref)
    acc2_ref[...] = jnp.zeros_like(acc2_ref)

    def scores(ki, s_ref):
        off = pl.multiple_of(ki * tk, tk)
        kb = k_ref[pl.ds(off, tk), :]
        bias = bias_ref[jnp.clip(ki - ratio * qi, -2, ratio + 1) + 2]
        s1 = _dot(kb, q1).astype(BF16) + bias
        s_ref[0] = s1
        s2 = _dot(kb, q2).astype(BF16) + bias
        s_ref[1] = s2
        return (jnp.max(s1, axis=0, keepdims=True).astype(F32),
                jnp.max(s2, axis=0, keepdims=True).astype(F32))

    def probs(s_ref, p_ref, mt, m):
        m_out, alpha_out = [], []
        for j in range(2):
            m_new = jnp.maximum(m[j], mt[j])
            alpha_out.append(jnp.exp2(m[j] - m_new))
            p_ref[j] = jnp.exp2(s_ref[j] - m_new.astype(BF16))
            m_out.append(m_new)
        return tuple(m_out), tuple(alpha_out)

    def add_values(ki, p_ref, alpha):
        off = pl.multiple_of(ki * tk, tk)
        vb = vt_ref[:, pl.ds(off, tk)]
        for j, acc_ref in enumerate((acc1_ref, acc2_ref)):
            acc_ref[...] = alpha[j] * acc_ref[...] + _dot(vb, p_ref[j])

    def tick(i, s_new, s_cur, p_new, p_cur, mt_next, alpha, m, last_scores=True):
        mt_after = scores(i + 2, s_new) if last_scores else None
        m, alpha_next = probs(s_cur, p_new, mt_next, m)
        add_values(i, p_cur, alpha)
        return mt_after, alpha_next, m

    def pair(j, carry):
        mt_next, alpha, m = carry
        i = 2 * j
        mt_next, alpha, m = tick(i, sa_ref, sb_ref, pb_ref, pa_ref, mt_next, alpha, m)
        mt_next, alpha, m = tick(i + 1, sb_ref, sa_ref, pa_ref, pb_ref, mt_next, alpha, m)
        return mt_next, alpha, m

    neg = jnp.full((1, tq), -jnp.inf, F32)
    mt0 = scores(0, sa_ref)
    mt1 = scores(1, sb_ref)
    m, alpha = probs(sa_ref, pa_ref, mt0, (neg, neg))
    mt_last, alpha, m = lax.fori_loop(0, nk // 2 - 1, pair, (mt1, alpha, m))
    _, alpha_last, m = tick(nk - 2, None, sb_ref, pb_ref, pa_ref, mt_last, alpha, m, last_scores=False)
    add_values(nk - 1, pb_ref, alpha_last)
    a1 = acc1_ref[...]
    a2 = acc2_ref[...]
    o = a1[:dv] / a1[dv:dv + 1] - lam_ref[0] * (a2[:dv] / a2[dv:dv + 1])
    ms = jnp.mean(o * o, axis=0, keepdims=True)
    o = o * lax.rsqrt(ms + LN_EPS) * g_ref[...] * out_scale
    o_ref[...] = o.T.astype(BF16)


def _attention(lam, qt, k, vt, bias, g_col, B, L, H, hd, tk, tq, lam_init):
    Da = H * 2 * hd
    nq = L // tq
    nk = L // tk
    n_cls = bias.shape[1]
    assert nk % 2 == 0 and tq % tk == 0
    dvp = vt.shape[2]
    k3 = k.reshape(B, L, Da)
    out = pl.pallas_call(
        functools.partial(_attn_kernel, tk=tk, tq=tq, nk=nk, hd=hd, out_scale=1.0 - lam_init),
        grid=(B, H, nq),
        in_specs=[
            pl.BlockSpec(memory_space=pltpu.SMEM),
            pl.BlockSpec((None, 2 * hd, tq), lambda b, h, q: (b, h, q)),
            pl.BlockSpec((None, L, 2 * hd), lambda b, h, q: (b, 0, h)),
            pl.BlockSpec((None, None, dvp, L), lambda b, h, q: (b, h, 0, 0)),
            pl.BlockSpec((None, n_cls, tk, tq), lambda b, h, q: (h, 0, 0, 0)),
            pl.BlockSpec((2 * hd, 1), lambda b, h, q: (0, 0)),
        ],
        out_specs=pl.BlockSpec((None, tq, 2 * hd), lambda b, h, q: (b, q, h)),
        out_shape=jax.ShapeDtypeStruct((B, L, Da), BF16),
        scratch_shapes=[pltpu.VMEM((dvp, tq), F32), pltpu.VMEM((dvp, tq), F32),
                        pltpu.VMEM((2, tk, tq), BF16), pltpu.VMEM((2, tk, tq), BF16),
                        pltpu.VMEM((2, tk, tq), BF16), pltpu.VMEM((2, tk, tq), BF16)],
        compiler_params=_cparams(("arbitrary", "arbitrary", "arbitrary")),
        name="diff_attn",
    )(lam, qt, k3, vt, bias, g_col)
    return out.reshape(B * L, Da)


def _proj_out_kernel(cy_ref, vx_ref, x0_ref, hb_ref, ya_ref, x_ref, woh_ref, woa_ref, g_ref, b_ref,
                     wrc_ref, wrh_ref, x1_ref, eid_ref, gate_ref, *, alpha, n_groups, per_group):
    yh = ((cy_ref[...] + vx_ref[...] * hb_ref[...]) * x0_ref[...].astype(F32)).astype(BF16)
    mix = _dot(yh, woh_ref[...]) + _dot(ya_ref[...], woa_ref[...])
    x1 = _layer_norm_rows(alpha * x_ref[...] + mix, g_ref[...], b_ref[...])
    x1_ref[...] = x1
    hi = x1.astype(BF16)
    lo = (x1 - hi.astype(F32)).astype(BF16)
    lg2 = _dot(hi, wrc_ref[...])
    lg = lg2[:, :LANES] + lg2[:, LANES:] + _dot(lo, wrh_ref[...])
    col = lax.broadcasted_iota(jnp.int32, lg.shape, 1).astype(F32)
    far = jnp.float32(LANES)
    neg_inf = jnp.float32(-jnp.inf)

    def first_col(mask):
        return jnp.min(jnp.where(mask, col, far), axis=-1, keepdims=True)

    gmask = col < n_groups
    glog = jnp.where(gmask, lg, neg_inf)
    gmax = jnp.max(glog, axis=-1, keepdims=True)
    g_sel = first_col(glog == gmax)
    g_w = 1.0 / jnp.sum(jnp.where(gmask, jnp.exp(glog - gmax), 0.0), axis=-1, keepdims=True)
    lo_col = n_groups + g_sel * per_group
    emask = (col >= lo_col) & (col < lo_col + per_group)
    elog = jnp.where(emask, lg, neg_inf)
    emax = jnp.max(elog, axis=-1, keepdims=True)
    eexp = jnp.where(emask, jnp.exp(elog - emax), 0.0)
    prob = eexp / jnp.sum(eexp, axis=-1, keepdims=True)
    p1 = jnp.max(prob, axis=-1, keepdims=True)
    i1 = first_col(emask & (prob == p1))
    mask2 = emask & (col != i1)
    p2 = jnp.max(jnp.where(mask2, prob, -1.0), axis=-1, keepdims=True)
    i2 = first_col(mask2 & (prob == p2))
    psum = p1 + p2
    eid = jnp.where(col == 0, i1 - n_groups, jnp.where(col == 1, i2 - n_groups, 0.0))
    eid_ref[...] = eid.astype(jnp.int32)
    gate_ref[...] = jnp.where(col == 0, g_w * p1 / psum, jnp.where(col == 1, g_w * p2 / psum, 0.0))


def _proj_out(cy, vx, x0c, hb, ya, x, woh, woa, g, b, wrc, wrh, alpha, n_groups, per_group, tm):
    T, D = x.shape
    Dh = cy.shape[1]
    Da = ya.shape[1]
    row = lambda i: (i, 0)
    fix = lambda i: (0, 0)
    return pl.pallas_call(
        functools.partial(_proj_out_kernel, alpha=alpha, n_groups=n_groups, per_group=per_group),
        grid=(T // tm,),
        in_specs=[
            pl.BlockSpec((tm, Dh), row),
            pl.BlockSpec((tm, Dh), row),
            pl.BlockSpec((tm, Dh), row),
            pl.BlockSpec((1, Dh), fix),
            pl.BlockSpec((tm, Da), row),
            pl.BlockSpec((tm, D), row),
            pl.BlockSpec((Dh, D), fix),
            pl.BlockSpec((Da, D), fix),
            pl.BlockSpec((1, D), fix),
            pl.BlockSpec((1, D), fix),
            pl.BlockSpec((D, 2 * LANES), fix),
            pl.BlockSpec((D, LANES), fix),
        ],
        out_specs=[
            pl.BlockSpec((tm, D), row),
            pl.BlockSpec((tm, LANES), row),
            pl.BlockSpec((tm, LANES), row),
        ],
        out_shape=[
            jax.ShapeDtypeStruct((T, D), F32),
            jax.ShapeDtypeStruct((T, LANES), jnp.int32),
            jax.ShapeDtypeStruct((T, LANES), F32),
        ],
        compiler_params=_cparams(("arbitrary",)),
        name="proj_out_ln_router",
    )(cy, vx, x0c, hb, ya, x, woh, woa, g, b, wrc, wrh)


def _moe_kernel(be_ref, nu_ref, wb_ref, wo_ref, nv_ref, src_ref, srcn_ref, dst_ref, x_hbm, wg_ref,
                wu_ref, wd_ref, out_hbm, xbuf, ybuf, wgb, wub, wdb, sem_in, sem_out,
                *, bm, chunk, n_real):
    i = pl.program_id(0)
    n_used = nu_ref[0]
    slot = i % 2
    off = wo_ref[i]
    off_next = wo_ref[jnp.minimum(i + 1, pl.num_programs(0) - 1)]
    n_valid = nv_ref[i]

    def gather_row(idx_ref, base, s, r):
        return pltpu.make_async_copy(x_hbm.at[pl.ds(idx_ref[0, base + r], 1), :],
                                     xbuf.at[s, pl.ds(r, 1), :], sem_in.at[s])

    def scatter_row(s, r):
        d = jnp.where(r < n_valid, dst_ref[0, off + r], n_real + s * bm + r)
        return pltpu.make_async_copy(ybuf.at[s, pl.ds(r, 1), :],
                                     out_hbm.at[pl.ds(d, 1), :], sem_out.at[s])

    def wait_gather(s):
        pltpu.make_async_copy(x_hbm.at[pl.ds(0, bm), :], xbuf.at[s], sem_in.at[s]).wait()

    def wait_scatter(s):
        pltpu.make_async_copy(ybuf.at[s], out_hbm.at[pl.ds(0, bm), :], sem_out.at[s]).wait()

    @pl.when(i == 0)
    def _():
        def body(r, c):
            gather_row(src_ref, off, 0, r).start()
            return c

        lax.fori_loop(0, bm, body, 0, unroll=8)

    def run(cur, nxt):
        changed = jnp.logical_or(i == 0, be_ref[i] != be_ref[jnp.maximum(i - 1, 0)])

        @pl.when(changed)
        def _():
            wgb[...] = wg_ref[...].astype(BF16)
            wub[...] = wu_ref[...].astype(BF16)
            wdb[...] = wd_ref[...].astype(BF16)

        wait_gather(cur)
        for r in range(bm):
            gather_row(srcn_ref, off_next, nxt, r).start()
        xb = xbuf[cur].astype(BF16)
        hg = _dot(xb, wgb[...])
        hu = _dot(xb, wub[...])
        act = ((hg / (1.0 + jnp.exp(-hg))) * hu).astype(BF16)
        for c in range(bm // chunk):
            rows = slice(c * chunk, (c + 1) * chunk)
            ybuf[cur, rows, :] = _dot(act[rows], wdb[...])
            for r in range(c * chunk, (c + 1) * chunk):
                scatter_row(cur, r).start()

        @pl.when(i > 0)
        def _():
            wait_scatter(nxt)

        @pl.when(i == n_used - 1)
        def _():
            wait_scatter(cur)
            wait_gather(nxt)

    for parity in (0, 1):
        @pl.when(jnp.logical_and(i < n_used, slot == parity))
        def _():
            run(parity, 1 - parity)


def _moe_experts(scalars, src_win, dst_win, x1, w_gate, w_up, w_down, n_rows_out, bm):
    n_blocks = scalars[0].shape[0]
    T, D = x1.shape
    E, _, De = w_gate.shape
    idx_block = (None, 1, 2 * bm)
    cur = lambda i, be, nu, wb, wo, nv: (wb[i], 0, 0)
    nxt = lambda i, be, nu, wb, wo, nv: (wb[jnp.minimum(i + 1, n_blocks - 1)], 0, 0)
    wsel = lambda i, be, nu, wb, wo, nv: (be[i], 0, 0)
    grid_spec = pltpu.PrefetchScalarGridSpec(
        num_scalar_prefetch=5,
        grid=(n_blocks,),
        in_specs=[
            pl.BlockSpec(idx_block, cur, memory_space=pltpu.SMEM),
            pl.BlockSpec(idx_block, nxt, memory_space=pltpu.SMEM),
            pl.BlockSpec(idx_block, cur, memory_space=pltpu.SMEM),
            pl.BlockSpec(memory_space=pl.ANY),
            pl.BlockSpec((None, D, De), wsel),
            pl.BlockSpec((None, D, De), wsel),
            pl.BlockSpec((None, De, D), wsel),
        ],
        out_specs=pl.BlockSpec(memory_space=pl.ANY),
        scratch_shapes=[
            pltpu.VMEM((2, bm, D), F32),
            pltpu.VMEM((2, bm, D), F32),
            pltpu.VMEM((D, De), BF16),
            pltpu.VMEM((D, De), BF16),
            pltpu.VMEM((De, D), BF16),
            pltpu.SemaphoreType.DMA((2,)),
            pltpu.SemaphoreType.DMA((2,)),
        ],
    )
    return pl.pallas_call(
        functools.partial(_moe_kernel, bm=bm, chunk=min(bm, 128), n_real=n_rows_out - 2 * bm),
        grid_spec=grid_spec,
        out_shape=jax.ShapeDtypeStruct((n_rows_out, D), F32),
        compiler_params=_cparams(("arbitrary",)),
        name="moe_experts",
    )(*scalars, src_win, src_win, dst_win, x1, w_gate, w_up, w_down)


def _combine_kernel(ya_ref, yb_ref, gate_ref, x1_ref, g_ref, b_ref, *o_refs, alpha, na):
    gates = gate_ref[...]
    y = gates[:, 0:1] * ya_ref[...] + gates[:, 1:2] * yb_ref[...]
    res = _layer_norm_rows(alpha * x1_ref[...] + y, g_ref[...], b_ref[...])
    if len(o_refs) == 1:
        o_refs[0][...] = res
    else:
        i = pl.program_id(0)

        @pl.when(i < na)
        def _():
            o_refs[0][...] = res

        @pl.when(i >= na)
        def _():
            o_refs[1][...] = res


def _combine_ln(y2, gates, x1, g, b, alpha, tm, split_rows=None):
    T, D = x1.shape
    nt = T // tm
    row = lambda i: (i, 0)
    fix = lambda i: (0, 0)
    if split_rows is None:
        na = nt
        out_specs = pl.BlockSpec((tm, D), row)
        out_shape = jax.ShapeDtypeStruct((T, D), F32)
    else:
        na = split_rows // tm
        out_specs = [pl.BlockSpec((tm, D), lambda i: (jnp.minimum(i, na - 1), 0)),
                     pl.BlockSpec((tm, D), lambda i: (jnp.maximum(i - na, 0), 0))]
        out_shape = [jax.ShapeDtypeStruct((split_rows, D), F32),
                     jax.ShapeDtypeStruct((T - split_rows, D), F32)]
    return pl.pallas_call(
        functools.partial(_combine_kernel, alpha=alpha, na=na),
        grid=(nt,),
        in_specs=[
            pl.BlockSpec((tm, D), row),
            pl.BlockSpec((tm, D), lambda i: (nt + i, 0)),
            pl.BlockSpec((tm, LANES), row),
            pl.BlockSpec((tm, D), row),
            pl.BlockSpec((1, D), fix),
            pl.BlockSpec((1, D), fix),
        ],
        out_specs=out_specs,
        out_shape=out_shape,
        compiler_params=_cparams(("arbitrary",)),
        name="moe_combine_ln",
    )(y2, y2, gates, x1, g, b)


def _hyena_filter(L, dh, f_w1, f_b1, f_freq, f_w2, f_b2, f_w3):
    emb = f_w1.shape[0]
    bands = (emb - 1) // 2
    t = jnp.linspace(0.0, 1.0, L, dtype=F32)[:, None]
    w = (2.0 * math.pi / L) * jnp.arange(L, dtype=F32)[:, None]
    f = jnp.linspace(1e-4, bands - 1, bands, dtype=F32)[None, :]
    z = jnp.concatenate([t, jnp.cos(f * w), -jnp.sin(f * w)], axis=-1)
    hp = lax.Precision.HIGHEST
    max_decay = math.log(DECAY_TARGET) / FAST_DECAY_PCT
    min_decay = math.log(DECAY_TARGET) / SLOW_DECAY_PCT
    deltas = jnp.abs(jnp.linspace(min_decay, max_decay, dh, dtype=F32))

    def taps(zz, tt, w3):
        h = jnp.sin(f_freq * (jnp.dot(zz, f_w1, precision=hp) + f_b1))
        h = jnp.sin(f_freq * (jnp.dot(h, f_w2, precision=hp) + f_b2))
        return jnp.dot(h, w3, precision=hp) * jnp.exp(-tt * deltas)

    h_fwd = taps(z, t, f_w3[:, :dh])
    h_bwd_rev = taps(z[::-1], t[::-1], f_w3[:, dh:])[:L - 1]
    k_full = jnp.concatenate([h_fwd, jnp.zeros((1, dh), F32), h_bwd_rev], axis=0)
    return k_full / jnp.sum(jnp.abs(k_full), axis=0, keepdims=True)


def _dispatch_tables(eid, n_experts, bm):
    T = eid.shape[0]
    A = T * TOP_K
    flat_e = eid.reshape(A)
    se, order = lax.sort((flat_e, jnp.arange(A, dtype=jnp.int32)), num_keys=1, is_stable=True)
    experts = jnp.arange(n_experts, dtype=jnp.int32)
    onehot = se[:, None] == experts[None, :]
    counts = jnp.sum(onehot, axis=0, dtype=jnp.int32)
    starts = jnp.cumsum(counts) - counts
    pcounts = (counts + bm - 1) // bm * bm
    pends = jnp.cumsum(pcounts)
    pstarts = pends - pcounts
    n_blocks = -(-A // bm) + n_experts
    first_row = jnp.arange(n_blocks, dtype=jnp.int32) * bm
    block_e = jnp.minimum(jnp.sum(pends[None, :] <= first_row[:, None], axis=1, dtype=jnp.int32),
                          n_experts - 1)
    n_used = (pends[-1] // bm).astype(jnp.int32).reshape(1)
    blk = jnp.arange(n_blocks, dtype=jnp.int32)
    within = blk - pstarts[block_e] // bm
    n_valid = jnp.clip(counts[block_e] - within * bm, 0, bm)
    a0 = jnp.where(n_valid > 0, starts[block_e] + within * bm, 0)
    src = order // TOP_K
    dst = (order % TOP_K) * T + order // TOP_K

    def windows(v):
        nb = -(-A // bm)
        v2 = jnp.pad(v, (0, (nb + 2) * bm - A)).reshape(nb + 2, bm)
        return jnp.concatenate([v2[:-1], v2[1:]], axis=1).reshape(nb + 1, 1, 2 * bm)

    return (block_e, n_used, a0 // bm, a0 % bm, n_valid), windows(src), windows(dst), A + 2 * bm


def _pick(n, pref):
    t = min(n, pref)
    assert n % t == 0, (n, t)
    return t


def _trunk(xa, xb, p):
    Ba, L, D = xa.shape
    Bb = xb.shape[0]
    assert xb.shape[1:] == (L, D)
    B = Ba + Bb
    T = B * L
    depth = p["w_in"].shape[0]
    dh = p["h_bias"].shape[1]
    H = p["rel_table"].shape[1]
    hd = p["lam_q1"].shape[1]
    da = H * 2 * hd
    n_groups = p["w_route_group"].shape[2]
    n_experts = p["w_route_expert"].shape[2]
    per_group = n_experts // n_groups
    alpha = (2.0 * depth) ** 0.25
    assert L % FFT_N2 == 0 and (L // FFT_N2) % 8 == 0
    assert 2 * hd == LANES and dh % LANES == 0 and n_groups + n_experts <= LANES

    tm = _pick(L, 1024)
    tk_attn = _pick(L, 512)
    tq_attn = _pick(L, 1024)
    bm = 512

    xs = _ln_embed(xa.reshape(Ba * L, D), xb.reshape(Bb * L, D), p["ln_emb_g"], p["ln_emb_b"], tm)
    tables = _dft_tables(L)
    nb = L // FFT_N2
    bias = _bias_tiles(p["rel_table"], tk_attn, tq_attn)

    for l in range(depth):
        lam_init = 0.8 - 0.6 * math.exp(-0.3 * l)
        w_in = p["w_in"][l]
        wh = w_in[:, :3 * dh].astype(BF16)
        wa = w_in[:, 3 * dh:]
        wqt = (wa[:, :da] * (hd ** -0.5 * LOG2E)).T.astype(BF16)
        wk = wa[:, da:2 * da].astype(BF16)
        wvt = wa[:, 2 * da:].T.astype(BF16)
        uh, k, qt, vt = _proj_in(xs, wh, wk, wqt, wvt, B, L, H, tm)

        vx, x0c = _hyena_pre(uh, p["conv_w"][l], p["conv_b"][l], B, L, tm)
        k_full = _hyena_filter(L, dh, p["f_w1"][l], p["f_b1"][l], p["f_freq"][l], p["f_w2"][l],
                               p["f_b2"][l], p["f_w3"][l])
        h_spec = _filter_spec(k_full.T.reshape(dh, 2 * nb, FFT_N2), tables, LANES)
        vx4 = vx.reshape(B, L, dh).transpose(0, 2, 1).reshape(B, dh, nb, FFT_N2)
        cy = _fft_conv(vx4, tables, h_spec, LANES)
        cy = cy.reshape(B, dh, L).transpose(0, 2, 1).reshape(T, dh)

        lam = (jnp.exp(jnp.sum(p["lam_q1"][l] * p["lam_k1"][l]))
               - jnp.exp(jnp.sum(p["lam_q2"][l] * p["lam_k2"][l])) + lam_init).reshape(1).astype(F32)
        ya = _attention(lam, qt, k, vt, bias, p["subln_g"][l].reshape(2 * hd, 1), B, L, H, hd, tk_attn,
                        tq_attn, lam_init)

        w_out = p["w_out"][l]
        w_r = jnp.concatenate([p["w_route_group"][l], p["w_route_expert"][l]], axis=1)
        w_r = jnp.pad(w_r, ((0, 0), (0, LANES - w_r.shape[1])))
        w_r_hi = w_r.astype(BF16)
        w_r_lo = (w_r - w_r_hi.astype(F32)).astype(BF16)
        x1, eid, gates = _proj_out(
            cy, vx, x0c, p["h_bias"][l].reshape(1, dh), ya, xs,
            w_out[:dh].astype(BF16), w_out[dh:].astype(BF16),
            p["ln1_g"][l].reshape(1, D), p["ln1_b"][l].reshape(1, D),
            jnp.concatenate([w_r_hi, w_r_lo], axis=1), w_r_hi,
            alpha, n_groups, per_group, tm)

        scalars, src_win, dst_win, n_rows = _dispatch_tables(eid[:, :TOP_K], n_experts, bm)
        y2 = _moe_experts(scalars, src_win, dst_win, x1, p["w_gate"][l], p["w_up"][l], p["w_down"][l],
                          n_rows, bm)
        xs = _combine_ln(y2, gates, x1, p["ln2_g"][l].reshape(1, D), p["ln2_b"][l].reshape(1, D),
                         alpha, tm, split_rows=Ba * L if l == depth - 1 else None)
    return xs[0].reshape(Ba, L, D), xs[1].reshape(Bb, L, D)


def kernel(x_prompt, x_sample, ln_emb_g, ln_emb_b, rel_table, w_in, conv_w, conv_b, f_w1, f_b1, f_freq,
           f_w2, f_b2, f_w3, h_bias, lam_q1, lam_k1, lam_q2, lam_k2, subln_g, w_out, ln1_g, ln1_b,
           w_route_group, w_route_expert, w_gate, w_up, w_down, ln2_g, ln2_b):
    p = dict(ln_emb_g=ln_emb_g, ln_emb_b=ln_emb_b, rel_table=rel_table, w_in=w_in, conv_w=conv_w,
             conv_b=conv_b, f_w1=f_w1, f_b1=f_b1, f_freq=f_freq, f_w2=f_w2, f_b2=f_b2, f_w3=f_w3,
             h_bias=h_bias, lam_q1=lam_q1, lam_k1=lam_k1, lam_q2=lam_q2, lam_k2=lam_k2,
             subln_g=subln_g, w_out=w_out, ln1_g=ln1_g, ln1_b=ln1_b, w_route_group=w_route_group,
             w_route_expert=w_route_expert, w_gate=w_gate, w_up=w_up, w_down=w_down, ln2_g=ln2_g,
             ln2_b=ln2_b)
    return _trunk(x_prompt, x_sample, p)
```
